```python
import math
import jax, jax.numpy as jnp
from jax import lax
import numpy as np


D_MODEL = 2048
BATCH = 8
SEQ = 4096
DEPTH = 4
DEC_BATCH = 16
DEC_SEQ = 2048
PAST_LEN = 128

HEAD_DIM = 128
N_HEADS_A = 6
N_KV_A = 2
GROUP_A = N_HEADS_A // N_KV_A
N_HEADS_B = 6
N_KV_B = 2
GROUP_B = N_HEADS_B // N_KV_B
N_HEADS_M = 4
N_MEM = 256
N_BRANCH = 3
WINDOW = 128
BLOCK = 128
GRID_W = 64
ROPE_THETA = 10000.0
ROPE_AXIS_DIM = HEAD_DIM // 2
NUM_BUCKETS = 32
MAX_DISTANCE = 128
N_EXPERTS = 16
CAPACITY_FACTOR = 2
D_FF_EXPERT = 4096
EPS = 1e-6
NEG = -1e30
ATTN_SCALE = HEAD_DIM ** -0.5

W_QA = N_HEADS_A * HEAD_DIM
W_KA = N_KV_A * HEAD_DIM
W_QB = N_HEADS_B * HEAD_DIM
W_KB = N_KV_B * HEAD_DIM
W_QM = N_HEADS_M * HEAD_DIM
W_GATES = N_BRANCH * D_MODEL
IN_COLS = W_QA + 2 * W_KA + W_QB + 2 * W_KB + W_QM + W_GATES
SPLITS = [W_QA, W_QA + W_KA, W_QA + 2 * W_KA,
          W_QA + 2 * W_KA + W_QB, W_QA + 2 * W_KA + W_QB + W_KB, W_QA + 2 * W_KA + W_QB + 2 * W_KB,
          W_QA + 2 * W_KA + W_QB + 2 * W_KB + W_QM]

kernel_name = 'hybrid_gated_encoder'


def rmsnorm(x, g):
    xf = x.astype(jnp.float32)
    y = xf * lax.rsqrt(jnp.mean(xf * xf, axis=-1, keepdims=True) + EPS)
    return y.astype(x.dtype) * g


def t5_bucket(rel):
    half = NUM_BUCKETS // 2
    max_exact = half // 2
    ret = jnp.where(rel > 0, half, 0)
    n = jnp.abs(rel)
    nf = jnp.maximum(n, 1).astype(jnp.float32)
    large = max_exact + (jnp.log(nf / max_exact) / math.log(MAX_DISTANCE / max_exact)
                         * (half - max_exact)).astype(jnp.int32)
    large = jnp.minimum(large, half - 1)
    return ret + jnp.where(n < max_exact, n, large)


def rope_tables(pos):
    freqs = ROPE_THETA ** (-jnp.arange(0, ROPE_AXIS_DIM, 2, dtype=jnp.float32) / ROPE_AXIS_DIM)
    ang = pos.astype(jnp.float32)[:, None] * freqs[None, :]
    return jnp.cos(ang)[None, :, None, :], jnp.sin(ang)[None, :, None, :]


def rope_axis(x, cos, sin):
    x1, x2 = jnp.split(x, 2, axis=-1)
    return jnp.concatenate([x1 * cos - x2 * sin, x1 * sin + x2 * cos], axis=-1)


def rope2d(x, row_tab, col_tab):
    xf = x.astype(jnp.float32)
    out = jnp.concatenate([rope_axis(xf[..., :ROPE_AXIS_DIM], *row_tab),
                           rope_axis(xf[..., ROPE_AXIS_DIM:], *col_tab)], axis=-1)
    return out.astype(x.dtype)


def window_attention(q, k, v, sink, rel_bias):
    B, S = q.shape[0], q.shape[1]
    nb = S // BLOCK
    qb = q.reshape(B, nb, BLOCK, N_KV_A, GROUP_A, HEAD_DIM)
    pad = ((0, 0), (BLOCK, BLOCK), (0, 0), (0, 0))
    kp = jnp.pad(k, pad).reshape(B, nb + 2, BLOCK, N_KV_A, HEAD_DIM)
    vp = jnp.pad(v, pad).reshape(B, nb + 2, BLOCK, N_KV_A, HEAD_DIM)
    kb = jnp.concatenate([kp[:, :-2], kp[:, 1:-1], kp[:, 2:]], axis=2)
    vb = jnp.concatenate([vp[:, :-2], vp[:, 1:-1], vp[:, 2:]], axis=2)
    rel = jnp.arange(3 * BLOCK)[None, :] - BLOCK - jnp.arange(BLOCK)[:, None]
    bias = rel_bias[t5_bucket(rel)].astype(jnp.float32)
    bias = bias.transpose(2, 0, 1).reshape(N_KV_A, GROUP_A, BLOCK, 3 * BLOCK)
    key_pos = jnp.arange(nb)[:, None, None] * BLOCK + (jnp.arange(3 * BLOCK) - BLOCK)[None, None, :]
    valid = (jnp.abs(rel) <= WINDOW)[None] & (key_pos >= 0) & (key_pos < S)
    s = jnp.einsum('bnqkgd,bnskd->bnkgqs', qb, kb).astype(jnp.float32) * ATTN_SCALE
    s = jnp.where(valid[None, :, None, None], s + bias[None, None], NEG)
    sk = sink.astype(jnp.float32).reshape(N_KV_A, GROUP_A)[None, None, :, :, None, None]
    m = jnp.maximum(jnp.max(s, axis=-1, keepdims=True), sk)
    p = jnp.exp(s - m)
    p = p / (jnp.sum(p, axis=-1, keepdims=True) + jnp.exp(sk - m))
    o = jnp.einsum('bnkgqs,bnskd->bnqkgd', p.astype(v.dtype), vb)
    return o.reshape(B, S, N_HEADS_A * HEAD_DIM)


def axial_attention(q, k, v, q_gain, k_gain, row_tab, col_tab):
    B, S = q.shape[0], q.shape[1]
    nb = S // BLOCK
    q = rope2d(rmsnorm(q, q_gain), row_tab, col_tab)
    k = rope2d(rmsnorm(k, k_gain), row_tab, col_tab)
    qb = q.reshape(B, nb, BLOCK, N_KV_B, GROUP_B, HEAD_DIM).transpose(1, 0, 2, 3, 4, 5)

    def one_block(qi):
        s = jnp.einsum('bqkgd,bskd->bkgqs', qi, k).astype(jnp.float32) * ATTN_SCALE
        p = jax.nn.softmax(s, axis=-1)
        return jnp.einsum('bkgqs,bskd->bqkgd', p.astype(v.dtype), v)

    o = lax.map(one_block, qb)
    return o.transpose(1, 0, 2, 3, 4, 5).reshape(B, S, N_HEADS_B * HEAD_DIM)


def memory_attention(q, km, vm):
    B, S = q.shape[0], q.shape[1]
    s = jnp.einsum('bshd,bmhd->bhsm', q, km).astype(jnp.float32) * ATTN_SCALE
    p = jax.nn.softmax(s, axis=-1)
    o = jnp.einsum('bhsm,bmhd->bshd', p.astype(vm.dtype), vm)
    return o.reshape(B, S, N_HEADS_M * HEAD_DIM)


def expert_choice_moe(h, w_router, w_gate_e, w_up_e, w_down_e):
    B, S, D = h.shape
    n_tok = B * S
    hf = h.reshape(n_tok, D)
    aff = jax.nn.softmax(jnp.matmul(hf, w_router).astype(jnp.float32), axis=-1)
    cap = max(1, CAPACITY_FACTOR * n_tok // N_EXPERTS)
    gate, idx = lax.top_k(aff.T, cap)
    xe = hf[idx]
    a = jnp.einsum('ecd,edf->ecf', xe, w_gate_e)
    b = jnp.einsum('ecd,edf->ecf', xe, w_up_e)
    ye = jnp.einsum('ecf,efd->ecd', jax.nn.silu(a) * b, w_down_e) * gate[..., None].astype(h.dtype)
    out = jnp.zeros((n_tok, D), h.dtype).at[idx.reshape(-1)].add(ye.reshape(-1, D))
    return out.reshape(B, S, D)


def encoder(x, mem, norm_attn, w_in, sink_a, q_norm_b, k_norm_b, norm_mem, w_mem_kv,
            w_branch_a, w_branch_b, w_branch_m, w_out, rel_bias, norm_ffn, w_router,
            w_gate_e, w_up_e, w_down_e, norm_final):
    B, S, D = x.shape
    rows = S // GRID_W
    row_pos = jnp.repeat(jnp.arange(rows), GRID_W)
    col_pos = jnp.tile(jnp.arange(GRID_W), rows)
    row_tab = rope_tables(row_pos)
    col_tab = rope_tables(col_pos)
    for l in range(DEPTH):
        h = rmsnorm(x, norm_attn[l])
        proj = jnp.matmul(h, w_in[l])
        qa, ka, va, qb, kb, vb, qm, gates = jnp.split(proj, SPLITS, axis=-1)
        o_a = window_attention(qa.reshape(B, S, N_HEADS_A, HEAD_DIM),
                               ka.reshape(B, S, N_KV_A, HEAD_DIM),
                               va.reshape(B, S, N_KV_A, HEAD_DIM), sink_a[l], rel_bias)
        o_b = axial_attention(qb.reshape(B, S, N_HEADS_B, HEAD_DIM),
                              kb.reshape(B, S, N_KV_B, HEAD_DIM),
                              vb.reshape(B, S, N_KV_B, HEAD_DIM),
                              q_norm_b[l], k_norm_b[l], row_tab, col_tab)
        mem_kv = jnp.matmul(rmsnorm(mem, norm_mem[l]), w_mem_kv[l])
        km, vm = jnp.split(mem_kv, 2, axis=-1)
        Bm, M = mem.shape[0], mem.shape[1]
        o_m = memory_attention(qm.reshape(B, S, N_HEADS_M, HEAD_DIM),
                               km.reshape(Bm, M, N_HEADS_M, HEAD_DIM),
                               vm.reshape(Bm, M, N_HEADS_M, HEAD_DIM))
        g = jax.nn.sigmoid(gates.astype(jnp.float32)).astype(x.dtype).reshape(B, S, N_BRANCH, D)
        merged = (g[:, :, 0] * jnp.matmul(o_a, w_branch_a[l])
                  + g[:, :, 1] * jnp.matmul(o_b, w_branch_b[l])
                  + g[:, :, 2] * jnp.matmul(o_m, w_branch_m[l]))
        x = x + jnp.matmul(merged, w_out[l])
        h = rmsnorm(x, norm_ffn[l])
        x = x + expert_choice_moe(h, w_router[l], w_gate_e[l], w_up_e[l], w_down_e[l])
    return rmsnorm(x, norm_final)


def setup_inputs(seed: int = 0) -> dict:
    key = jax.random.key(seed)
    ks = jax.random.split(key, 24)

    def nrm(k, shape, scale):
        return jax.random.normal(k, shape, jnp.float32) * scale

    def gain(k, shape):
        return 1.0 + 0.02 * jax.random.normal(k, shape, jnp.float32)

    return {
        'x_prompt': nrm(ks[0], (BATCH, SEQ, D_MODEL), 1.0),
        'x_sample': nrm(ks[1], (DEC_BATCH, DEC_SEQ, D_MODEL), 1.0),
        'mem_prompt': nrm(ks[2], (BATCH, N_MEM, D_MODEL), 1.0),
        'mem_sample': nrm(ks[3], (DEC_BATCH, N_MEM, D_MODEL), 1.0),
        'norm_attn': gain(ks[4], (DEPTH, D_MODEL)),
        'w_in': nrm(ks[5], (DEPTH, D_MODEL, IN_COLS), D_MODEL ** -0.5),
        'sink_a': nrm(ks[6], (DEPTH, N_HEADS_A), 0.5),
        'q_norm_b': gain(ks[7], (DEPTH, HEAD_DIM)),
        'k_norm_b': gain(ks[8], (DEPTH, HEAD_DIM)),
        'norm_mem': gain(ks[9], (DEPTH, D_MODEL)),
        'w_mem_kv': nrm(ks[10], (DEPTH, D_MODEL, 2 * N_HEADS_M * HEAD_DIM), D_MODEL ** -0.5),
        'w_branch_a': nrm(ks[11], (DEPTH, W_QA, D_MODEL), W_QA ** -0.5),
        'w_branch_b': nrm(ks[12], (DEPTH, W_QB, D_MODEL), W_QB ** -0.5),
        'w_branch_m': nrm(ks[13], (DEPTH, W_QM, D_MODEL), W_QM ** -0.5),
        'w_out': nrm(ks[14], (DEPTH, D_MODEL, D_MODEL), D_MODEL ** -0.5),
        'rel_bias': nrm(ks[15], (NUM_BUCKETS, N_HEADS_A), 0.5),
        'norm_ffn': gain(ks[16], (DEPTH, D_MODEL)),
        'w_router': nrm(ks[17], (DEPTH, D_MODEL, N_EXPERTS), D_MODEL ** -0.5),
        'w_gate_e': nrm(ks[18], (DEPTH, N_EXPERTS, D_MODEL, D_FF_EXPERT), D_MODEL ** -0.5),
        'w_up_e': nrm(ks[19], (DEPTH, N_EXPERTS, D_MODEL, D_FF_EXPERT), D_MODEL ** -0.5),
        'w_down_e': nrm(ks[20], (DEPTH, N_EXPERTS, D_FF_EXPERT, D_MODEL), D_FF_EXPERT ** -0.5),
        'norm_final': gain(ks[21], (D_MODEL,)),
    }


def reference(x_prompt, x_sample, mem_prompt, mem_sample, norm_attn, w_in, sink_a, q_norm_b,
              k_norm_b, norm_mem, w_mem_kv, w_branch_a, w_branch_b, w_branch_m, w_out, rel_bias,
              norm_ffn, w_router, w_gate_e, w_up_e, w_down_e, norm_final):
    y_prompt = encoder(x_prompt, mem_prompt, norm_attn, w_in, sink_a, q_norm_b, k_norm_b, norm_mem,
                       w_mem_kv, w_branch_a, w_branch_b, w_branch_m, w_out, rel_bias, norm_ffn,
                       w_router, w_gate_e, w_up_e, w_down_e, norm_final)
    y_sample = encoder(x_sample, mem_sample, norm_attn, w_in, sink_a, q_norm_b, k_norm_b, norm_mem,
                       w_mem_kv, w_branch_a, w_branch_b, w_branch_m, w_out, rel_bias, norm_ffn,
                       w_router, w_gate_e, w_up_e, w_down_e, norm_final)
    return (y_prompt, y_sample)
```

```python
import functools
import math

import jax
import jax.numpy as jnp
from jax import lax
from jax.experimental import pallas as pl
from jax.experimental.pallas import tpu as pltpu

F32 = jnp.float32
BF16 = jnp.bfloat16

HEAD_DIM = 128
N_HEADS_A = 6
N_KV_A = 2
GROUP_A = N_HEADS_A // N_KV_A
N_HEADS_B = 6
N_KV_B = 2
GROUP_B = N_HEADS_B // N_KV_B
N_HEADS_M = 4
N_BRANCH = 3
WINDOW = 128
BLOCK = 128
GRID_W = 64
ROPE_THETA = 10000.0
ROPE_AXIS_DIM = HEAD_DIM // 2
NUM_BUCKETS = 32
MAX_DISTANCE = 128
CAPACITY_FACTOR = 2
EPS = 1e-6
NEG = -1e30
ATTN_SCALE = HEAD_DIM ** -0.5

W_QA = N_HEADS_A * HEAD_DIM
W_KA = N_KV_A * HEAD_DIM
W_QB = N_HEADS_B * HEAD_DIM
W_KB = N_KV_B * HEAD_DIM
W_QM = N_HEADS_M * HEAD_DIM

MIB = 1024 * 1024


def _tile(n, pref):
    t = min(n, pref)
    assert n % t == 0, (n, t)
    return t


def _params(sem, vmem_mib):
    return pltpu.CompilerParams(dimension_semantics=sem, vmem_limit_bytes=vmem_mib * MIB)


def _dot_t(a, b):
    return lax.dot_general(a, b, (((1,), (1,)), ((), ())), preferred_element_type=F32)


def _norm_matmul_kernel(x_ref, g_ref, w_ref, o_ref, h_ref):
    @pl.when(pl.program_id(1) == 0)
    def _():
        x = x_ref[...]
        ms = jnp.mean(x * x, axis=-1, keepdims=True)
        h_ref[...] = (x * lax.rsqrt(ms + EPS) * g_ref[...]).astype(h_ref.dtype)

    o_ref[...] = jnp.dot(h_ref[...], w_ref[...], preferred_element_type=F32).astype(o_ref.dtype)


def _norm_matmul(x, g, w):
    n, d = x.shape
    m = w.shape[1]
    tm = _tile(n, 1024)
    tn = _tile(m, 1024)
    return pl.pallas_call(
        _norm_matmul_kernel,
        grid=(n // tm, m // tn),
        in_specs=[
            pl.BlockSpec((tm, d), lambda i, j: (i, 0)),
            pl.BlockSpec((1, d), lambda i, j: (0, 0)),
            pl.BlockSpec((d, tn), lambda i, j: (0, j)),
        ],
        out_specs=pl.BlockSpec((tm, tn), lambda i, j: (i, j)),
        out_shape=jax.ShapeDtypeStruct((n, m), BF16),
        scratch_shapes=[pltpu.VMEM((tm, d), BF16)],
        compiler_params=_params(("parallel", "arbitrary"), 48),
        name="norm_matmul",
    )(x, g.reshape(1, d), w)


def _proj_layout(d):
    off = {}
    o = 0
    for name, width in (("gates", N_BRANCH * d), ("qa", W_QA), ("qb", W_QB), ("ka", W_KA), ("va", W_KA),
                        ("kb", W_KB), ("vb", W_KB), ("qm", W_QM)):
        off[name] = o
        o += width
    off["total"] = o
    return off


def _reorder_w_in(w_in, d):
    s = [0]
    for width in (W_QA, W_KA, W_KA, W_QB, W_KB, W_KB, W_QM, N_BRANCH * d):
        s.append(s[-1] + width)
    qa, ka, va, qb, kb, vb, qm, gates = (w_in[..., s[i]:s[i + 1]] for i in range(8))
    return jnp.concatenate([gates, qa, qb, ka, va, kb, vb, qm], axis=-1)


def _qk_prep_kernel(q_ref, k_ref, cos_ref, sin_ref, qg_ref, kg_ref, qo_ref, ko_ref):
    cos = cos_ref[...]
    sin = sin_ref[...]
    lane = lax.broadcasted_iota(jnp.int32, cos.shape, 1)
    first = (lane % ROPE_AXIS_DIM) < (ROPE_AXIS_DIM // 2)

    def prep(x, g, scale):
        x = x.astype(F32)
        y = x * lax.rsqrt(jnp.mean(x * x, axis=-1, keepdims=True) + EPS) * g
        half = ROPE_AXIS_DIM // 2
        partner = jnp.where(first, pltpu.roll(y, HEAD_DIM - half, 1), pltpu.roll(y, half, 1))
        r = y * cos + partner * sin
        if scale is not None:
            r = r * scale
        return r.astype(BF16)

    for h in range(N_HEADS_B):
        sl = slice(h * HEAD_DIM, (h + 1) * HEAD_DIM)
        qo_ref[:, sl] = prep(q_ref[:, sl], qg_ref[...], ATTN_SCALE)
    for h in range(N_KV_B):
        sl = slice(h * HEAD_DIM, (h + 1) * HEAD_DIM)
        ko_ref[:, sl] = prep(k_ref[:, sl], kg_ref[...], None)


def _rope_tables(seq):
    pos = jnp.arange(seq)
    row_pos = (pos // GRID_W).astype(F32)
    col_pos = (pos % GRID_W).astype(F32)
    freqs = ROPE_THETA ** (-jnp.arange(0, ROPE_AXIS_DIM, 2, dtype=F32) / ROPE_AXIS_DIM)
    ar = row_pos[:, None] * freqs[None, :]
    ac = col_pos[:, None] * freqs[None, :]
    cos = jnp.concatenate([jnp.cos(ar), jnp.cos(ar), jnp.cos(ac), jnp.cos(ac)], axis=-1)
    sin = jnp.concatenate([-jnp.sin(ar), jnp.sin(ar), -jnp.sin(ac), jnp.sin(ac)], axis=-1)
    return cos, sin


def _qk_prep(proj, cos, sin, q_gain, k_gain, seq, off):
    n = proj.shape[0]
    tm = _tile(seq, 1024)
    ns = seq // tm
    qblk = off["qb"] // W_QB
    kblk = off["kb"] // W_KB
    assert qblk * W_QB == off["qb"] and kblk * W_KB == off["kb"]
    return pl.pallas_call(
        _qk_prep_kernel,
        grid=(n // tm,),
        in_specs=[
            pl.BlockSpec((tm, W_QB), lambda i: (i, qblk)),
            pl.BlockSpec((tm, W_KB), lambda i: (i, kblk)),
            pl.BlockSpec((tm, HEAD_DIM), lambda i: (i % ns, 0)),
            pl.BlockSpec((tm, HEAD_DIM), lambda i: (i % ns, 0)),
            pl.BlockSpec((1, HEAD_DIM), lambda i: (0, 0)),
            pl.BlockSpec((1, HEAD_DIM), lambda i: (0, 0)),
        ],
        out_specs=[
            pl.BlockSpec((tm, W_QB), lambda i: (i, 0)),
            pl.BlockSpec((tm, W_KB), lambda i: (i, 0)),
        ],
        out_shape=[jax.ShapeDtypeStruct((n, W_QB), BF16), jax.ShapeDtypeStruct((n, W_KB), BF16)],
        compiler_params=_params(("parallel",), 32),
        name="qk_prep",
    )(proj, proj, cos, sin, q_gain.reshape(1, HEAD_DIM), k_gain.reshape(1, HEAD_DIM))


def _t5_bucket(rel):
    half = NUM_BUCKETS // 2
    max_exact = half // 2
    ret = jnp.where(rel > 0, half, 0)
    n = jnp.abs(rel)
    nf = jnp.maximum(n, 1).astype(F32)
    large = max_exact + (jnp.log(nf / max_exact) / math.log(MAX_DISTANCE / max_exact)
                         * (half - max_exact)).astype(jnp.int32)
    large = jnp.minimum(large, half - 1)
    return ret + jnp.where(n < max_exact, n, large)


def _window_bias(rel_bias):
    rel = jnp.arange(3 * BLOCK)[None, :] - BLOCK - jnp.arange(BLOCK)[:, None]
    return rel_bias[_t5_bucket(rel)].astype(F32).transpose(2, 0, 1)


def _window_kernel(sink_ref, q_ref, kp_ref, kc_ref, kn_ref, vp_ref, vc_ref, vn_ref, bias_ref, o_ref, *, nb):
    n = pl.program_id(1)
    row = lax.broadcasted_iota(jnp.int32, (BLOCK, 3 * BLOCK), 0)
    col = lax.broadcasted_iota(jnp.int32, (BLOCK, 3 * BLOCK), 1)
    rel = col - BLOCK - row
    valid = (jnp.abs(rel) <= WINDOW) & ((col >= BLOCK) | (n > 0)) & ((col < 2 * BLOCK) | (n < nb - 1))
    for kv in range(N_KV_A):
        ks = slice(kv * HEAD_DIM, (kv + 1) * HEAD_DIM)
        k = jnp.concatenate([kp_ref[:, ks], kc_ref[:, ks], kn_ref[:, ks]], axis=0)
        v = jnp.concatenate([vp_ref[:, ks], vc_ref[:, ks], vn_ref[:, ks]], axis=0)
        for g in range(GROUP_A):
            h = kv * GROUP_A + g
            hs = slice(h * HEAD_DIM, (h + 1) * HEAD_DIM)
            s = _dot_t(q_ref[:, hs], k) * ATTN_SCALE
            s = jnp.where(valid, s + bias_ref[h], NEG)
            sk = sink_ref[h]
            m = jnp.maximum(jnp.max(s, axis=-1, keepdims=True), sk)
            p = jnp.exp(s - m)
            p = p / (jnp.sum(p, axis=-1, keepdims=True) + jnp.exp(sk - m))
            o = jnp.dot(p.astype(BF16), v, preferred_element_type=F32)
            o_ref[:, hs] = o.astype(o_ref.dtype)


def _window_attention(proj, sink, bias, batch, seq, off):
    n = proj.shape[0]
    nb = seq // BLOCK
    qblk = off["qa"] // W_QA
    kblk = off["ka"] // W_KA
    vblk = off["va"] // W_KA
    assert qblk * W_QA == off["qa"] and kblk * W_KA == off["ka"] and vblk * W_KA == off["va"]

    def cur(col):
        return lambda b, i: (b * nb + i, col)

    def prev(col):
        return lambda b, i: (b * nb + jnp.maximum(i - 1, 0), col)

    def nxt(col):
        return lambda b, i: (b * nb + jnp.minimum(i + 1, nb - 1), col)

    return pl.pallas_call(
        functools.partial(_window_kernel, nb=nb),
        grid=(batch, nb),
        in_specs=[
            pl.BlockSpec(memory_space=pltpu.SMEM),
            pl.BlockSpec((BLOCK, W_QA), cur(qblk)),
            pl.BlockSpec((BLOCK, W_KA), prev(kblk)),
            pl.BlockSpec((BLOCK, W_KA), cur(kblk)),
            pl.BlockSpec((BLOCK, W_KA), nxt(kblk)),
            pl.BlockSpec((BLOCK, W_KA), prev(vblk)),
            pl.BlockSpec((BLOCK, W_KA), cur(vblk)),
            pl.BlockSpec((BLOCK, W_KA), nxt(vblk)),
            pl.BlockSpec((N_HEADS_A, BLOCK, 3 * BLOCK), lambda b, i: (0, 0, 0)),
        ],
        out_specs=pl.BlockSpec((BLOCK, W_QA), lambda b, i: (b * nb + i, 0)),
        out_shape=jax.ShapeDtypeStruct((n, W_QA), BF16),
        compiler_params=_params(("parallel", "parallel"), 32),
        name="window_attention",
    )(sink, proj, proj, proj, proj, proj, proj, proj, bias)


def _axial_kernel(q_ref, k_ref, v_ref, o_ref, q3_ref, m_ref, l_ref, acc_ref, *, tq):
    j = pl.program_id(3)

    @pl.when(j == 0)
    def _():
        for g in range(GROUP_B):
            q3_ref[g * tq:(g + 1) * tq, :] = q_ref[:, g * HEAD_DIM:(g + 1) * HEAD_DIM]
        m_ref[...] = jnp.full(m_ref.shape, -jnp.inf, F32)
        l_ref[...] = jnp.zeros(l_ref.shape, F32)
        acc_ref[...] = jnp.zeros(acc_ref.shape, F32)

    s = _dot_t(q3_ref[...], k_ref[...])
    m_prev = m_ref[...]
    m_new = jnp.maximum(m_prev, jnp.max(s, axis=-1, keepdims=True))
    alpha = jnp.exp(m_prev - m_new)
    p = jnp.exp(s - m_new)
    l_ref[...] = alpha * l_ref[...] + jnp.sum(p, axis=-1, keepdims=True)
    acc_ref[...] = alpha * acc_ref[...] + jnp.dot(p.astype(BF16), v_ref[...], preferred_element_type=F32)
    m_ref[...] = m_new

    @pl.when(j == pl.num_programs(3) - 1)
    def _():
        o = acc_ref[...] / l_ref[...]
        for g in range(GROUP_B):
            o_ref[:, g * HEAD_DIM:(g + 1) * HEAD_DIM] = o[g * tq:(g + 1) * tq, :].astype(o_ref.dtype)


def _axial_attention(q_rot, k_rot, proj, batch, seq, off):
    n = proj.shape[0]
    tq = _tile(seq, 512)
    tk = _tile(seq, 512)
    nq = seq // tq
    nk = seq // tk
    vblk = off["vb"] // HEAD_DIM
    gw = GROUP_B * HEAD_DIM
    return pl.pallas_call(
        functools.partial(_axial_kernel, tq=tq),
        grid=(batch, N_KV_B, nq, nk),
        in_specs=[
            pl.BlockSpec((tq, gw), lambda b, kv, i, j: (b * nq + i, kv)),
            pl.BlockSpec((tk, HEAD_DIM), lambda b, kv, i, j: (b * nk + j, kv)),
            pl.BlockSpec((tk, HEAD_DIM), lambda b, kv, i, j: (b * nk + j, vblk + kv)),
        ],
        out_specs=pl.BlockSpec((tq, gw), lambda b, kv, i, j: (b * nq + i, kv)),
        out_shape=jax.ShapeDtypeStruct((n, W_QB), BF16),
        scratch_shapes=[
            pltpu.VMEM((GROUP_B * tq, HEAD_DIM), BF16),
            pltpu.VMEM((GROUP_B * tq, 1), F32),
            pltpu.VMEM((GROUP_B * tq, 1), F32),
            pltpu.VMEM((GROUP_B * tq, HEAD_DIM), F32),
        ],
        compiler_params=_params(("parallel", "parallel", "parallel", "arbitrary"), 48),
        name="axial_attention",
    )(q_rot, k_rot, proj)


def _mem_kernel(q_ref, k_ref, v_ref, o_ref):
    for h in range(N_HEADS_M):
        hs = slice(h * HEAD_DIM, (h + 1) * HEAD_DIM)
        s = _dot_t(q_ref[:, hs], k_ref[:, hs]) * ATTN_SCALE
        m = jnp.max(s, axis=-1, keepdims=True)
        p = jnp.exp(s - m)
        p = p / jnp.sum(p, axis=-1, keepdims=True)
        o = jnp.dot(p.astype(BF16), v_ref[:, hs], preferred_element_type=F32)
        o_ref[:, hs] = o.astype(o_ref.dtype)


def _memory_attention(proj, mem_kv, batch, seq, n_mem, off):
    n = proj.shape[0]
    tq = _tile(seq, 1024)
    nq = seq // tq
    qblk = off["qm"] // W_QM
    assert qblk * W_QM == off["qm"]
    return pl.pallas_call(
        _mem_kernel,
        grid=(batch, nq),
        in_specs=[
            pl.BlockSpec((tq, W_QM), lambda b, i: (b * nq + i, qblk)),
            pl.BlockSpec((n_mem, W_QM), lambda b, i: (b, 0)),
            pl.BlockSpec((n_mem, W_QM), lambda b, i: (b, 1)),
        ],
        out_specs=pl.BlockSpec((tq, W_QM), lambda b, i: (b * nq + i, 0)),
        out_shape=jax.ShapeDtypeStruct((n, W_QM), BF16),
        compiler_params=_params(("parallel", "parallel"), 32),
        name="memory_attention",
    )(proj, mem_kv, mem_kv)


def _merge_kernel(oa_ref, ob_ref, om_ref, g0_ref, g1_ref, g2_ref, x_ref, wa_ref, wb_ref, wm_ref, wo_ref,
                  nf_ref, wr_ref, xo_ref, h_ref, aff_ref, merged_ref, *, chunk):
    d = x_ref.shape[1]
    for c in range(d // chunk):
        sl = slice(c * chunk, (c + 1) * chunk)
        t = jax.nn.sigmoid(g0_ref[:, sl].astype(F32)) * jnp.dot(
            oa_ref[...], wa_ref[:, sl], preferred_element_type=F32)
        t = t + jax.nn.sigmoid(g1_ref[:, sl].astype(F32)) * jnp.dot(
            ob_ref[...], wb_ref[:, sl], preferred_element_type=F32)
        t = t + jax.nn.sigmoid(g2_ref[:, sl].astype(F32)) * jnp.dot(
            om_ref[...], wm_ref[:, sl], preferred_element_type=F32)
        merged_ref[:, sl] = t.astype(merged_ref.dtype)
    xn = x_ref[...] + jnp.dot(merged_ref[...], wo_ref[...], preferred_element_type=F32)
    xo_ref[...] = xn
    h = xn * lax.rsqrt(jnp.mean(xn * xn, axis=-1, keepdims=True) + EPS) * nf_ref[...]
    h_ref[...] = h
    logits = lax.dot_general(wr_ref[...], h, (((1,), (1,)), ((), ())), preferred_element_type=F32,
                             precision=lax.Precision.HIGHEST)
    e = jnp.exp(logits - jnp.max(logits, axis=0, keepdims=True))
    aff_ref[...] = e / jnp.sum(e, axis=0, keepdims=True)


def _merge_out(o_a, o_b, o_m, proj, x, wa, wb, wm, wo, norm_ffn, w_router_t):
    n, d = x.shape
    n_exp = w_router_t.shape[0]
    tm = _tile(n, 256)
    const = lambda i: (0, 0)
    resident = dict(pipeline_mode=pl.Buffered(1))
    return pl.pallas_call(
        functools.partial(_merge_kernel, chunk=_tile(d, 512)),
        grid=(n // tm,),
        in_specs=[
            pl.BlockSpec((tm, W_QA), lambda i: (i, 0)),
            pl.BlockSpec((tm, W_QB), lambda i: (i, 0)),
            pl.BlockSpec((tm, W_QM), lambda i: (i, 0)),
            pl.BlockSpec((tm, d), lambda i: (i, 0)),
            pl.BlockSpec((tm, d), lambda i: (i, 1)),
            pl.BlockSpec((tm, d), lambda i: (i, 2)),
            pl.BlockSpec((tm, d), lambda i: (i, 0)),
            pl.BlockSpec((W_QA, d), const, **resident),
            pl.BlockSpec((W_QB, d), const, **resident),
            pl.BlockSpec((W_QM, d), const, **resident),
            pl.BlockSpec((d, d), const, **resident),
            pl.BlockSpec((1, d), const),
            pl.BlockSpec((n_exp, d), const),
        ],
        out_specs=[
            pl.BlockSpec((tm, d), lambda i: (i, 0)),
            pl.BlockSpec((tm, d), lambda i: (i, 0)),
            pl.BlockSpec((n_exp, tm), lambda i: (0, i)),
        ],
        out_shape=[
            jax.ShapeDtypeStruct((n, d), F32),
            jax.ShapeDtypeStruct((n, d), F32),
            jax.ShapeDtypeStruct((n_exp, n), F32),
        ],
        scratch_shapes=[pltpu.VMEM((tm, d), BF16)],
        input_output_aliases={6: 0},
        compiler_params=_params(("parallel",), 56),
        name="merge_out",
    )(o_a, o_b, o_m, proj, proj, proj, x, wa, wb, wm, wo, norm_ffn.reshape(1, d), w_router_t)


def _moe_kernel(idx_ref, gate_ref, wg_ref, wu_ref, wd_ref, h_hbm, x_in_hbm, x_hbm, xe_f32, xe_bf16, xrow, acc,
                sem_h, sem_x, sem_s, *, tc, nc):
    del x_in_hbm
    e = pl.program_id(0)
    c = pl.program_id(1)
    f = pl.program_id(2)
    base = (e * nc + c) * tc

    @pl.when(f == 0)
    def _():
        def issue(k, carry):
            t = idx_ref[base + k]
            pltpu.make_async_copy(h_hbm.at[pl.ds(t, 1)], xe_f32.at[pl.ds(k, 1)], sem_h).start()
            pltpu.make_async_copy(x_hbm.at[pl.ds(t, 1)], xrow.at[pl.ds(k, 1)], sem_x).start()
            return carry

        lax.fori_loop(0, tc, issue, 0)
        pltpu.make_async_copy(h_hbm.at[pl.ds(0, tc)], xe_f32, sem_h).wait()
        xe_bf16[...] = xe_f32[...].astype(BF16)
        acc[...] = jnp.zeros(acc.shape, F32)

    xe = xe_bf16[...]
    a = jnp.dot(xe, wg_ref[...], preferred_element_type=F32)
    b = jnp.dot(xe, wu_ref[...], preferred_element_type=F32)
    hmid = (jax.nn.silu(a) * b).astype(BF16)
    acc[...] += jnp.dot(hmid, wd_ref[...], preferred_element_type=F32)

    @pl.when(f == pl.num_programs(2) - 1)
    def _():
        pltpu.make_async_copy(x_hbm.at[pl.ds(0, tc)], xrow, sem_x).wait()
        xrow[...] = xrow[...] + acc[...] * gate_ref[...]

        def scatter(k, carry):
            t = idx_ref[base + k]
            pltpu.make_async_copy(xrow.at[pl.ds(k, 1)], x_hbm.at[pl.ds(t, 1)], sem_s).start()
            return carry

        lax.fori_loop(0, tc, scatter, 0)
        pltpu.make_async_copy(xrow, x_hbm.at[pl.ds(0, tc)], sem_s).wait()


def _moe(x, h, idx, gate, wg, wu, wd):
    n, d = x.shape
    n_exp, cap = idx.shape
    dff = wg.shape[2]
    tc = _tile(cap, 1024)
    tf = _tile(dff, 512)
    nc = cap // tc
    return pl.pallas_call(
        functools.partial(_moe_kernel, tc=tc, nc=nc),
        grid_spec=pltpu.PrefetchScalarGridSpec(
            num_scalar_prefetch=1,
            grid=(n_exp, nc, dff // tf),
            in_specs=[
                pl.BlockSpec((tc, 1), lambda e, c, f, idx: (e * nc + c, 0)),
                pl.BlockSpec((None, d, tf), lambda e, c, f, idx: (e, 0, f)),
                pl.BlockSpec((None, d, tf), lambda e, c, f, idx: (e, 0, f)),
                pl.BlockSpec((None, tf, d), lambda e, c, f, idx: (e, f, 0)),
                pl.BlockSpec(memory_space=pl.ANY),
                pl.BlockSpec(memory_space=pl.ANY),
            ],
            out_specs=pl.BlockSpec(memory_space=pl.ANY),
            scratch_shapes=[
                pltpu.VMEM((tc, d), F32),
                pltpu.VMEM((tc, d), BF16),
                pltpu.VMEM((tc, d), F32),
                pltpu.VMEM((tc, d), F32),
                pltpu.SemaphoreType.DMA(()),
                pltpu.SemaphoreType.DMA(()),
                pltpu.SemaphoreType.DMA(()),
            ],
        ),
        out_shape=jax.ShapeDtypeStruct((n, d), F32),
        input_output_aliases={6: 0},
        compiler_params=_params(("arbitrary", "arbitrary", "arbitrary"), 56),
        name="moe_experts",
    )(idx.reshape(-1), gate.reshape(-1, 1), wg, wu, wd, h, x)


def _rmsnorm_kernel(x_ref, g_ref, o_ref):
    x = x_ref[...]
    o_ref[...] = x * lax.rsqrt(jnp.mean(x * x, axis=-1, keepdims=True) + EPS) * g_ref[...]


def _rmsnorm(x, g):
    n, d = x.shape
    tm = _tile(n, 1024)
    return pl.pallas_call(
        _rmsnorm_kernel,
        grid=(n // tm,),
        in_specs=[pl.BlockSpec((tm, d), lambda i: (i, 0)), pl.BlockSpec((1, d), lambda i: (0, 0))],
        out_specs=pl.BlockSpec((tm, d), lambda i: (i, 0)),
        out_shape=jax.ShapeDtypeStruct((n, d), F32),
        compiler_params=_params(("parallel",), 48),
        name="final_rmsnorm",
    )(x, g.reshape(1, d))


def _encoder(x, mem, w, bias):
    batch, seq, d = x.shape
    n_mem = mem.shape[1]
    n = batch * seq
    depth = w["w_in"].shape[0]
    n_exp = w["w_router_t"].shape[1]
    cap = max(1, CAPACITY_FACTOR * n // n_exp)
    off = _proj_layout(d)
    cos, sin = _rope_tables(seq)
    x = x.reshape(n, d)
    mem = mem.reshape(batch * n_mem, d)
    for l in range(depth):
        proj = _norm_matmul(x, w["norm_attn"][l], w["w_in"][l])
        o_a = _window_attention(proj, w["sink_a"][l], bias, batch, seq, off)
        q_rot, k_rot = _qk_prep(proj, cos, sin, w["q_norm_b"][l], w["k_norm_b"][l], seq, off)
        o_b = _axial_attention(q_rot, k_rot, proj, batch, seq, off)
        mem_kv = _norm_matmul(mem, w["norm_mem"][l], w["w_mem_kv"][l])
        o_m = _memory_attention(proj, mem_kv, batch, seq, n_mem, off)
        x, h, aff_t = _merge_out(o_a, o_b, o_m, proj, x, w["w_branch_a"][l], w["w_branch_b"][l],
                                 w["w_branch_m"][l], w["w_out"][l], w["norm_ffn"][l], w["w_router_t"][l])
        gate, idx = lax.top_k(aff_t, cap)
        x = _moe(x, h, idx.astype(jnp.int32), gate, w["w_gate_e"][l], w["w_up_e"][l], w["w_down_e"][l])
    return _rmsnorm(x, w["norm_final"]).reshape(batch, seq, d)


def kernel(x_prompt, x_sample, mem_prompt, mem_sample, norm_attn, w_in, sink_a, q_norm_b, k_norm_b, norm_mem,
           w_mem_kv, w_branch_a, w_branch_b, w_branch_m, w_out, rel_bias, norm_ffn, w_router, w_gate_e, w_up_e,
           w_down_e, norm_final):
    d = x_prompt.shape[-1]
    w = dict(
        norm_attn=norm_attn, sink_a=sink_a, q_norm_b=q_norm_b, k_norm_b=k_norm_b, norm_mem=norm_mem,
        norm_ffn=norm_ffn, norm_final=norm_final,
        w_in=_reorder_w_in(w_in, d).astype(BF16),
        w_mem_kv=w_mem_kv.astype(BF16),
        w_branch_a=w_branch_a.astype(BF16), w_branch_b=w_branch_b.astype(BF16),
        w_branch_m=w_branch_m.astype(BF16), w_out=w_out.astype(BF16),
        w_router_t=jnp.swapaxes(w_router, 1, 2),
        w_gate_e=w_gate_e.astype(BF16), w_up_e=w_up_e.astype(BF16), w_down_e=w_down_e.astype(BF16),
    )
    bias = _window_bias(rel_bias)
    y_prompt = _encoder(x_prompt, mem_prompt, w, bias)
    y_sample = _encoder(x_sample, mem_sample, w, bias)
    return (y_prompt, y_sample)
```

```python
import functools
import math

import jax
import jax.numpy as jnp
from jax import lax
from jax.experimental import pallas as pl
from jax.experimental.pallas import tpu as pltpu

F32 = jnp.float32
BF16 = jnp.bfloat16
U32 = jnp.uint32

HEAD_DIM = 128
N_HEADS_A = 6
N_KV_A = 2
GROUP_A = N_HEADS_A // N_KV_A
N_HEADS_B = 6
N_KV_B = 2
GROUP_B = N_HEADS_B // N_KV_B
N_HEADS_M = 4
N_BRANCH = 3
WINDOW = 128
BLOCK = 128
GRID_W = 64
ROPE_THETA = 10000.0
ROPE_AXIS_DIM = HEAD_DIM // 2
NUM_BUCKETS = 32
MAX_DISTANCE = 128
CAPACITY_FACTOR = 2
EPS = 1e-6
NEG = -1e30
ATTN_SCALE = HEAD_DIM ** -0.5
LOG2E = math.log2(math.e)

W_QA = N_HEADS_A * HEAD_DIM
W_KA = N_KV_A * HEAD_DIM
W_QB = N_HEADS_B * HEAD_DIM
W_KB = N_KV_B * HEAD_DIM
W_QM = N_HEADS_M * HEAD_DIM

MIB = 1024 * 1024
CHUNK = 512
SUBLANES = 8
DMA_UNROLL = 16


def _tile(n, pref):
    t = min(n, pref)
    assert n % t == 0, (n, t)
    return t


def _params(sem, vmem_mib):
    return pltpu.CompilerParams(dimension_semantics=sem, vmem_limit_bytes=vmem_mib * MIB)


def _dot(a, b):
    return jnp.dot(a, b, preferred_element_type=F32)


def _dot_t(a, b):
    return lax.dot_general(a, b, (((1,), (1,)), ((), ())), preferred_element_type=F32)


def _rms(x, g):
    return x * lax.rsqrt(jnp.mean(x * x, axis=-1, keepdims=True) + EPS) * g


def _norm_matmul_kernel(x_ref, g_ref, w_ref, o_ref, h_ref):
    @pl.when(pl.program_id(1) == 0)
    def _():
        h_ref[...] = _rms(x_ref[...], g_ref[...]).astype(h_ref.dtype)

    o_ref[...] = _dot(h_ref[...], w_ref[...]).astype(o_ref.dtype)


def _norm_matmul(x, g, w, l):
    n, d = x.shape
    m = w.shape[2]
    tm = _tile(n, 1024)
    tn = _tile(m, 1024)
    return pl.pallas_call(
        _norm_matmul_kernel,
        grid=(n // tm, m // tn),
        in_specs=[
            pl.BlockSpec((tm, d), lambda i, j: (i, 0)),
            pl.BlockSpec((None, 1, d), lambda i, j: (l, 0, 0)),
            pl.BlockSpec((None, d, tn), lambda i, j: (l, 0, j)),
        ],
        out_specs=pl.BlockSpec((tm, tn), lambda i, j: (i, j)),
        out_shape=jax.ShapeDtypeStruct((n, m), BF16),
        scratch_shapes=[pltpu.VMEM((tm, d), BF16)],
        compiler_params=_params(("parallel", "arbitrary"), 48),
        name="norm_matmul",
    )(x, g, w)


def _proj_layout(d):
    off = {}
    o = 0
    for name, width in (("gates", N_BRANCH * d), ("qa", W_QA), ("qb", W_QB), ("ka", W_KA), ("va", W_KA),
                        ("kb", W_KB), ("vb", W_KB), ("qm", W_QM)):
        off[name] = o
        o += width
    off["total"] = o
    return off


def _col_block(off, name, width):
    assert off[name] % width == 0, (name, width)
    return off[name] // width


def _reorder_w_in(w_in, d):
    s = [0]
    for width in (W_QA, W_KA, W_KA, W_QB, W_KB, W_KB, W_QM, N_BRANCH * d):
        s.append(s[-1] + width)
    qa, ka, va, qb, kb, vb, qm, gates = (w_in[..., s[i]:s[i + 1]] for i in range(8))
    return jnp.concatenate([gates, qa, qb, ka, va, kb, vb, qm], axis=-1)


def _attn_prep_kernel(q_ref, k_ref, va_ref, vb_ref, cos_ref, sin_ref, qg_ref, kg_ref, qo_ref, ko_ref, vat_ref,
                      vbt_ref):
    cos = cos_ref[...]
    sin = sin_ref[...]
    lane = lax.broadcasted_iota(jnp.int32, cos.shape, 1)
    half = ROPE_AXIS_DIM // 2
    first = (lane % ROPE_AXIS_DIM) < half

    def prep(x, g, scale):
        y = _rms(x.astype(F32), g)
        partner = jnp.where(first, pltpu.roll(y, HEAD_DIM - half, 1), pltpu.roll(y, half, 1))
        r = y * cos + partner * sin
        if scale is not None:
            r = r * scale
        return r.astype(BF16)

    for h in range(N_HEADS_B):
        sl = slice(h * HEAD_DIM, (h + 1) * HEAD_DIM)
        qo_ref[:, sl] = prep(q_ref[:, sl], qg_ref[...], ATTN_SCALE * LOG2E)
    for h in range(N_KV_B):
        sl = slice(h * HEAD_DIM, (h + 1) * HEAD_DIM)
        ko_ref[:, sl] = prep(k_ref[:, sl], kg_ref[...], None)
        vbt_ref[h] = vb_ref[:, sl].astype(F32).T.astype(BF16)
    for h in range(N_KV_A):
        sl = slice(h * HEAD_DIM, (h + 1) * HEAD_DIM)
        vat_ref[h] = va_ref[:, sl].astype(F32).T.astype(BF16)


def _rope_tables(seq):
    pos = jnp.arange(seq)
    row_pos = (pos // GRID_W).astype(F32)
    col_pos = (pos % GRID_W).astype(F32)
    freqs = ROPE_THETA ** (-jnp.arange(0, ROPE_AXIS_DIM, 2, dtype=F32) / ROPE_AXIS_DIM)
    ar = row_pos[:, None] * freqs[None, :]
    ac = col_pos[:, None] * freqs[None, :]
    cos = jnp.concatenate([jnp.cos(ar), jnp.cos(ar), jnp.cos(ac), jnp.cos(ac)], axis=-1)
    sin = jnp.concatenate([-jnp.sin(ar), jnp.sin(ar), -jnp.sin(ac), jnp.sin(ac)], axis=-1)
    return cos, sin


def _attn_prep(proj, cos, sin, q_gain, k_gain, l, batch, seq, off):
    n = proj.shape[0]
    tm = CHUNK
    assert seq % tm == 0
    ns = seq // tm
    vt_shape = jax.ShapeDtypeStruct((batch, N_KV_B, ns, HEAD_DIM, tm), BF16)
    vt_spec = pl.BlockSpec((None, N_KV_B, None, HEAD_DIM, tm), lambda i: (i // ns, 0, i % ns, 0, 0))
    return pl.pallas_call(
        _attn_prep_kernel,
        grid=(n // tm,),
        in_specs=[
            pl.BlockSpec((tm, W_QB), lambda i: (i, _col_block(off, "qb", W_QB))),
            pl.BlockSpec((tm, W_KB), lambda i: (i, _col_block(off, "kb", W_KB))),
            pl.BlockSpec((tm, W_KA), lambda i: (i, _col_block(off, "va", W_KA))),
            pl.BlockSpec((tm, W_KB), lambda i: (i, _col_block(off, "vb", W_KB))),
            pl.BlockSpec((tm, HEAD_DIM), lambda i: (i % ns, 0)),
            pl.BlockSpec((tm, HEAD_DIM), lambda i: (i % ns, 0)),
            pl.BlockSpec((None, 1, HEAD_DIM), lambda i: (l, 0, 0)),
            pl.BlockSpec((None, 1, HEAD_DIM), lambda i: (l, 0, 0)),
        ],
        out_specs=[
            pl.BlockSpec((tm, W_QB), lambda i: (i, 0)),
            pl.BlockSpec((tm, W_KB), lambda i: (i, 0)),
            vt_spec,
            vt_spec,
        ],
        out_shape=[jax.ShapeDtypeStruct((n, W_QB), BF16), jax.ShapeDtypeStruct((n, W_KB), BF16), vt_shape, vt_shape],
        compiler_params=_params(("parallel",), 32),
        name="attn_prep",
    )(proj, proj, proj, proj, cos, sin, q_gain, k_gain)


def _t5_bucket(rel):
    half = NUM_BUCKETS // 2
    max_exact = half // 2
    ret = jnp.where(rel > 0, half, 0)
    n = jnp.abs(rel)
    nf = jnp.maximum(n, 1).astype(F32)
    large = max_exact + (jnp.log(nf / max_exact) / math.log(MAX_DISTANCE / max_exact)
                         * (half - max_exact)).astype(jnp.int32)
    large = jnp.minimum(large, half - 1)
    return ret + jnp.where(n < max_exact, n, large)


def _window_bias_t(rel_bias):
    rel = jnp.arange(3 * BLOCK)[None, :] - BLOCK - jnp.arange(BLOCK)[:, None]
    bias = rel_bias[_t5_bucket(rel)].astype(F32)
    bias = bias.reshape(BLOCK, 3 * BLOCK, N_KV_A, GROUP_A).transpose(2, 1, 3, 0)
    return bias.reshape(N_KV_A, 3 * BLOCK, GROUP_A * BLOCK)


def _window_kernel(sink_ref, q_ref, kp_ref, kc_ref, kn_ref, vtp_ref, vtc_ref, vtn_ref, bias_ref, o_ref, *, nb, nsb):
    i = pl.program_id(1)
    w3 = 3 * BLOCK
    gw = GROUP_A * BLOCK
    key = lax.broadcasted_iota(jnp.int32, (w3, gw), 0)
    qpos = lax.broadcasted_iota(jnp.int32, (w3, gw), 1) % BLOCK
    band = jnp.abs(key - BLOCK - qpos) <= WINDOW
    for kv in range(N_KV_A):
        ks = slice(kv * HEAD_DIM, (kv + 1) * HEAD_DIM)
        kwin = jnp.concatenate([kp_ref[:, ks], kc_ref[:, ks], kn_ref[:, ks]], axis=0)
        vtwin = jnp.concatenate([vtp_ref[kv], vtc_ref[kv], vtn_ref[kv]], axis=1)
        sk = sink_ref[kv]
        for sb in range(nsb):
            gb = i * nsb + sb
            valid = band & ((key >= BLOCK) | (gb > 0)) & ((key < 2 * BLOCK) | (gb < nb - 1))
            rows = slice(sb * BLOCK, (sb + 1) * BLOCK)
            heads = [kv * GROUP_A + g for g in range(GROUP_A)]
            q3 = jnp.concatenate([q_ref[rows, h * HEAD_DIM:(h + 1) * HEAD_DIM] for h in heads], axis=0)
            st = _dot_t(kwin[sb * BLOCK:sb * BLOCK + w3], q3) * ATTN_SCALE
            st = jnp.where(valid, st + bias_ref[kv], NEG)
            m = jnp.maximum(jnp.max(st, axis=0, keepdims=True), sk)
            p = jnp.exp(st - m)
            inv = 1.0 / (jnp.sum(p, axis=0, keepdims=True) + jnp.exp(sk - m))
            ot = _dot(vtwin[:, sb * BLOCK:sb * BLOCK + w3], (p * inv).astype(BF16))
            for g, h in enumerate(heads):
                o_ref[rows, h * HEAD_DIM:(h + 1) * HEAD_DIM] = ot[:, g * BLOCK:(g + 1) * BLOCK].T.astype(o_ref.dtype)


def _window_attention(proj, va_t, sink_t, bias_t, batch, seq, off):
    n = proj.shape[0]
    tq = CHUNK
    nsb = tq // BLOCK
    nb = seq // BLOCK
    nt = seq // tq
    qblk = _col_block(off, "qa", W_QA)
    kblk = _col_block(off, "ka", W_KA)
    gw = GROUP_A * BLOCK
    return pl.pallas_call(
        functools.partial(_window_kernel, nb=nb, nsb=nsb),
        grid=(batch, nt),
        in_specs=[
            pl.BlockSpec((N_KV_A, 1, gw), lambda b, i: (0, 0, 0)),
            pl.BlockSpec((tq, W_QA), lambda b, i: (b * nt + i, qblk)),
            pl.BlockSpec((BLOCK, W_KA), lambda b, i: (b * nb + jnp.maximum(i * nsb - 1, 0), kblk)),
            pl.BlockSpec((tq, W_KA), lambda b, i: (b * nt + i, kblk)),
            pl.BlockSpec((BLOCK, W_KA), lambda b, i: (b * nb + jnp.minimum(i * nsb + nsb, nb - 1), kblk)),
            pl.BlockSpec((None, N_KV_A, None, HEAD_DIM, BLOCK),
                         lambda b, i: (b, 0, jnp.maximum(i - 1, 0), 0, nsb - 1)),
            pl.BlockSpec((None, N_KV_A, None, HEAD_DIM, tq), lambda b, i: (b, 0, i, 0, 0)),
            pl.BlockSpec((None, N_KV_A, None, HEAD_DIM, BLOCK),
                         lambda b, i: (b, 0, jnp.minimum(i + 1, nt - 1), 0, 0)),
            pl.BlockSpec((N_KV_A, 3 * BLOCK, gw), lambda b, i: (0, 0, 0)),
        ],
        out_specs=pl.BlockSpec((tq, W_QA), lambda b, i: (b * nt + i, 0)),
        out_shape=jax.ShapeDtypeStruct((n, W_QA), BF16),
        compiler_params=_params(("parallel", "parallel"), 32),
        name="window_attention",
    )(sink_t, proj, proj, proj, proj, va_t, va_t, va_t, bias_t)


def _axial_kernel(q_ref, k_ref, vt_ref, o_ref, q3_ref, m_ref, l_ref, acc_ref, *, tq, tk, nk):
    for g in range(GROUP_B):
        q3_ref[g * tq:(g + 1) * tq, :] = q_ref[:, g * HEAD_DIM:(g + 1) * HEAD_DIM]
    m_ref[...] = jnp.full(m_ref.shape, -jnp.inf, F32)
    l_ref[...] = jnp.zeros(l_ref.shape, F32)
    acc_ref[...] = jnp.zeros(acc_ref.shape, F32)

    def body(j, carry):
        ks = pl.multiple_of(j * tk, tk)
        st = _dot_t(k_ref[pl.ds(ks, tk), :], q3_ref[...])
        m_prev = m_ref[...]
        m_new = jnp.maximum(m_prev, jnp.max(st, axis=0, keepdims=True))
        alpha = jnp.exp2(m_prev - m_new)
        pt = jnp.exp2(st - m_new)
        l_ref[...] = alpha * l_ref[...] + jnp.sum(pt, axis=0, keepdims=True)
        acc_ref[...] = alpha * acc_ref[...] + _dot(vt_ref[j], pt.astype(BF16))
        m_ref[...] = m_new
        return carry

    lax.fori_loop(0, nk, body, 0)
    ot = acc_ref[...] * (1.0 / l_ref[...])
    for g in range(GROUP_B):
        o_ref[:, g * HEAD_DIM:(g + 1) * HEAD_DIM] = ot[:, g * tq:(g + 1) * tq].T.astype(o_ref.dtype)


def _axial_attention(q_rot, k_rot, vb_t, batch, seq):
    n = q_rot.shape[0]
    tq = _tile(seq, 512)
    tk = CHUNK
    nq = seq // tq
    nk = seq // tk
    gw = GROUP_B * HEAD_DIM
    rows = GROUP_B * tq
    return pl.pallas_call(
        functools.partial(_axial_kernel, tq=tq, tk=tk, nk=nk),
        grid=(batch, N_KV_B, nq),
        in_specs=[
            pl.BlockSpec((tq, gw), lambda b, kv, i: (b * nq + i, kv)),
            pl.BlockSpec((seq, HEAD_DIM), lambda b, kv, i: (b, kv)),
            pl.BlockSpec((None, None, nk, HEAD_DIM, tk), lambda b, kv, i: (b, kv, 0, 0, 0)),
        ],
        out_specs=pl.BlockSpec((tq, gw), lambda b, kv, i: (b * nq + i, kv)),
        out_shape=jax.ShapeDtypeStruct((n, W_QB), BF16),
        scratch_shapes=[
            pltpu.VMEM((rows, HEAD_DIM), BF16),
            pltpu.VMEM((1, rows), F32),
            pltpu.VMEM((1, rows), F32),
            pltpu.VMEM((HEAD_DIM, rows), F32),
        ],
        compiler_params=_params(("parallel", "parallel", "parallel"), 48),
        name="axial_attention",
    )(q_rot, k_rot, vb_t)


def _mem_kernel(q_ref, k_ref, v_ref, o_ref):
    for h in range(N_HEADS_M):
        hs = slice(h * HEAD_DIM, (h + 1) * HEAD_DIM)
        s = _dot_t(q_ref[:, hs], k_ref[:, hs]) * ATTN_SCALE
        m = jnp.max(s, axis=-1, keepdims=True)
        p = jnp.exp(s - m)
        p = p / jnp.sum(p, axis=-1, keepdims=True)
        o_ref[:, hs] = _dot(p.astype(BF16), v_ref[:, hs]).astype(o_ref.dtype)


def _memory_attention(proj, mem_kv, batch, seq, n_mem, off):
    n = proj.shape[0]
    tq = _tile(seq, 1024)
    nq = seq // tq
    qblk = _col_block(off, "qm", W_QM)
    return pl.pallas_call(
        _mem_kernel,
        grid=(batch, nq),
        in_specs=[
            pl.BlockSpec((tq, W_QM), lambda b, i: (b * nq + i, qblk)),
            pl.BlockSpec((n_mem, W_QM), lambda b, i: (b, 0)),
            pl.BlockSpec((n_mem, W_QM), lambda b, i: (b, 1)),
        ],
        out_specs=pl.BlockSpec((tq, W_QM), lambda b, i: (b * nq + i, 0)),
        out_shape=jax.ShapeDtypeStruct((n, W_QM), BF16),
        compiler_params=_params(("parallel", "parallel"), 32),
        name="memory_attention",
    )(proj, mem_kv, mem_kv)


def _sigmoid(x):
    return 0.5 * jnp.tanh(0.5 * x) + 0.5


def _branch_merge_kernel(oa_ref, ob_ref, om_ref, g0_ref, g1_ref, g2_ref, wa_ref, wb_ref, wm_ref, o_ref):
    t = _sigmoid(g0_ref[...].astype(F32)) * _dot(oa_ref[...], wa_ref[...])
    t = t + _sigmoid(g1_ref[...].astype(F32)) * _dot(ob_ref[...], wb_ref[...])
    t = t + _sigmoid(g2_ref[...].astype(F32)) * _dot(om_ref[...], wm_ref[...])
    o_ref[...] = t.astype(o_ref.dtype)


def _branch_merge(o_a, o_b, o_m, proj, wa, wb, wm, l, d):
    n = o_a.shape[0]
    tm = _tile(n, 1024)
    tn = _tile(d, 512)
    nj = d // tn
    return pl.pallas_call(
        _branch_merge_kernel,
        grid=(n // tm, nj),
        in_specs=[
            pl.BlockSpec((tm, W_QA), lambda i, j: (i, 0)),
            pl.BlockSpec((tm, W_QB), lambda i, j: (i, 0)),
            pl.BlockSpec((tm, W_QM), lambda i, j: (i, 0)),
            pl.BlockSpec((tm, tn), lambda i, j: (i, j)),
            pl.BlockSpec((tm, tn), lambda i, j: (i, nj + j)),
            pl.BlockSpec((tm, tn), lambda i, j: (i, 2 * nj + j)),
            pl.BlockSpec((None, W_QA, tn), lambda i, j: (l, 0, j)),
            pl.BlockSpec((None, W_QB, tn), lambda i, j: (l, 0, j)),
            pl.BlockSpec((None, W_QM, tn), lambda i, j: (l, 0, j)),
        ],
        out_specs=pl.BlockSpec((tm, tn), lambda i, j: (i, j)),
        out_shape=jax.ShapeDtypeStruct((n, d), BF16),
        compiler_params=_params(("parallel", "arbitrary"), 48),
        name="branch_merge",
    )(o_a, o_b, o_m, proj, proj, proj, wa, wb, wm)


def _out_proj_kernel(m_ref, x_ref, wo_ref, nf_ref, wr_ref, xo_ref, hp_ref, aff_ref):
    n_exp = aff_ref.shape[0]
    half = hp_ref.shape[1]
    xn = x_ref[...] + _dot(m_ref[...], wo_ref[...])
    xo_ref[...] = xn
    h = _rms(xn, nf_ref[...])
    h_hi = h.astype(BF16)
    h_hi32 = h_hi.astype(F32)
    h_lo = (h - h_hi32).astype(BF16)
    r_hi = _dot_t(wr_ref[...], h_hi)
    r_lo = _dot_t(wr_ref[...], h_lo)
    logits = r_hi[:n_exp] + r_hi[n_exp:] + r_lo[:n_exp]
    e = jnp.exp(logits - jnp.max(logits, axis=0, keepdims=True))
    aff_ref[...] = e / jnp.sum(e, axis=0, keepdims=True)
    bits = lax.bitcast_convert_type(h_hi32, U32)
    hp_ref[...] = (bits[:, :half] >> 16) | (bits[:, half:] & jnp.uint32(0xFFFF0000))


def _out_proj(merged, x, wo, norm_ffn, wr_split, l):
    n, d = x.shape
    n_exp = wr_split.shape[1] // 2
    tm = _tile(n, 512)
    return pl.pallas_call(
        _out_proj_kernel,
        grid=(n // tm,),
        in_specs=[
            pl.BlockSpec((tm, d), lambda i: (i, 0)),
            pl.BlockSpec((tm, d), lambda i: (i, 0)),
            pl.BlockSpec((None, d, d), lambda i: (l, 0, 0), pipeline_mode=pl.Buffered(1)),
            pl.BlockSpec((None, 1, d), lambda i: (l, 0, 0)),
            pl.BlockSpec((None, 2 * n_exp, d), lambda i: (l, 0, 0)),
        ],
        out_specs=[
            pl.BlockSpec((tm, d), lambda i: (i, 0)),
            pl.BlockSpec((tm, d // 2), lambda i: (i, 0)),
            pl.BlockSpec((n_exp, tm), lambda i: (0, i)),
        ],
        out_shape=[
            jax.ShapeDtypeStruct((n, d), F32),
            jax.ShapeDtypeStruct((n, d // 2), U32),
            jax.ShapeDtypeStruct((n_exp, n), F32),
        ],
        input_output_aliases={1: 0},
        compiler_params=_params(("parallel",), 56),
        name="out_proj",
    )(merged, x, wo, norm_ffn, wr_split)


def _moe_kernel(idx_ref, gate_ref, wg_ref, wu_ref, wd_ref, hp_hbm, x_in_hbm, x_hbm, xe_pack, xe_bf16, xrow, acc,
                sem_h, sem_x, sem_s, *, tc, nc, nf):
    del x_in_hbm
    e = pl.program_id(0)
    c = pl.program_id(1)
    f = pl.program_id(2)
    t = e * nc + c
    last_t = pl.num_programs(0) * nc - 1
    slot = t % 2
    half = xe_pack.shape[3]
    groups = tc // SUBLANES
    per_step = tc // nf

    def token(pos):
        r = idx_ref[pos]
        return lax.shift_right_logical(r, 3), r & (SUBLANES - 1)

    def h_copy(pos, dst_slot, kg, ks):
        rg, rs = token(pos)
        return pltpu.make_async_copy(hp_hbm.at[rg, pl.ds(rs, 1)], xe_pack.at[dst_slot, kg, pl.ds(ks, 1)],
                                     sem_h.at[dst_slot])

    def wait_h(dst_slot):
        pltpu.make_async_copy(hp_hbm.at[pl.ds(0, groups)], xe_pack.at[dst_slot], sem_h.at[dst_slot]).wait()

    @pl.when(f == 0)
    def _():
        @pl.when(t == 0)
        def _():
            def body(i, carry):
                for j in range(DMA_UNROLL):
                    h_copy(i * DMA_UNROLL + j, 0, i * (DMA_UNROLL // SUBLANES) + j // SUBLANES, j % SUBLANES).start()
                return carry

            lax.fori_loop(0, tc // DMA_UNROLL, body, 0)

        @pl.when(t > 0)
        def _():
            pltpu.make_async_copy(xrow, x_hbm.at[pl.ds(0, groups)], sem_s).wait()

        wait_h(slot)
        w = xe_pack[slot]
        lo = lax.bitcast_convert_type(w << 16, F32)
        hi = lax.bitcast_convert_type(w & jnp.uint32(0xFFFF0000), F32)
        xe_bf16[:, :half] = lo.reshape(tc, half).astype(BF16)
        xe_bf16[:, half:] = hi.reshape(tc, half).astype(BF16)
        acc[...] = jnp.zeros(acc.shape, F32)

    next_base = jnp.minimum(t + 1, last_t) * tc + f * per_step
    this_base = t * tc + f * per_step
    for j in range(per_step):
        kg = f * (per_step // SUBLANES) + j // SUBLANES
        h_copy(next_base + j, 1 - slot, kg, j % SUBLANES).start()
        rg, rs = token(this_base + j)
        pltpu.make_async_copy(x_hbm.at[rg, pl.ds(rs, 1)], xrow.at[kg, pl.ds(j % SUBLANES, 1)], sem_x).start()

    xe = xe_bf16[...]
    a = _dot(xe, wg_ref[...])
    b = _dot(xe, wu_ref[...])
    acc[...] += _dot((jax.nn.silu(a) * b).astype(BF16), wd_ref[...])

    @pl.when(f == nf - 1)
    def _():
        pltpu.make_async_copy(x_hbm.at[pl.ds(0, groups)], xrow, sem_x).wait()
        xrow[...] = xrow[...] + (acc[...] * gate_ref[...]).reshape(xrow.shape)

        def body(i, carry):
            for j in range(DMA_UNROLL):
                rg, rs = token(t * tc + i * DMA_UNROLL + j)
                kg = i * (DMA_UNROLL // SUBLANES) + j // SUBLANES
                pltpu.make_async_copy(xrow.at[kg, pl.ds(j % SUBLANES, 1)], x_hbm.at[rg, pl.ds(rs, 1)],
                                      sem_s).start(priority=j % 2)
            return carry

        lax.fori_loop(0, tc // DMA_UNROLL, body, 0)

        @pl.when(t == last_t)
        def _():
            pltpu.make_async_copy(xrow, x_hbm.at[pl.ds(0, groups)], sem_s).wait()
            wait_h(1 - slot)


def _moe(x, h_pack, idx, gate, wg, wu, wd, l):
    n, d = x.shape
    n_exp, cap = idx.shape
    dff = wg.shape[3]
    tc = _tile(cap, 1024)
    tf = _tile(dff, 512)
    nc = cap // tc
    nf = dff // tf
    assert n % SUBLANES == 0 and tc % DMA_UNROLL == 0 and DMA_UNROLL % SUBLANES == 0
    assert tc % nf == 0 and (tc // nf) % SUBLANES == 0
    x3 = x.reshape(n // SUBLANES, SUBLANES, d)
    hp3 = h_pack.reshape(n // SUBLANES, SUBLANES, d // 2)
    out = pl.pallas_call(
        functools.partial(_moe_kernel, tc=tc, nc=nc, nf=nf),
        grid_spec=pltpu.PrefetchScalarGridSpec(
            num_scalar_prefetch=1,
            grid=(n_exp, nc, nf),
            in_specs=[
                pl.BlockSpec((tc, 1), lambda e, c, f, idx: (e * nc + c, 0)),
                pl.BlockSpec((None, None, d, tf), lambda e, c, f, idx: (l, e, 0, f)),
                pl.BlockSpec((None, None, d, tf), lambda e, c, f, idx: (l, e, 0, f)),
                pl.BlockSpec((None, None, tf, d), lambda e, c, f, idx: (l, e, f, 0)),
                pl.BlockSpec(memory_space=pl.ANY),
                pl.BlockSpec(memory_space=pl.ANY),
            ],
            out_specs=pl.BlockSpec(memory_space=pl.ANY),
            scratch_shapes=[
                pltpu.VMEM((2, tc // SUBLANES, SUBLANES, d // 2), U32),
                pltpu.VMEM((tc, d), BF16),
                pltpu.VMEM((tc // SUBLANES, SUBLANES, d), F32),
                pltpu.VMEM((tc, d), F32),
                pltpu.SemaphoreType.DMA((2,)),
                pltpu.SemaphoreType.DMA(()),
                pltpu.SemaphoreType.DMA(()),
            ],
        ),
        out_shape=jax.ShapeDtypeStruct(x3.shape, F32),
        input_output_aliases={6: 0},
        compiler_params=_params(("arbitrary", "arbitrary", "arbitrary"), 56),
        name="moe_experts",
    )(idx.reshape(-1), gate.reshape(-1, 1), wg, wu, wd, hp3, x3)
    return out.reshape(n, d)


def _rmsnorm_kernel(x_ref, g_ref, o_ref):
    o_ref[...] = _rms(x_ref[...], g_ref[...])


def _rmsnorm(x, g):
    n, d = x.shape
    tm = _tile(n, 1024)
    return pl.pallas_call(
        _rmsnorm_kernel,
        grid=(n // tm,),
        in_specs=[pl.BlockSpec((tm, d), lambda i: (i, 0)), pl.BlockSpec((1, d), lambda i: (0, 0))],
        out_specs=pl.BlockSpec((tm, d), lambda i: (i, 0)),
        out_shape=jax.ShapeDtypeStruct((n, d), F32),
        compiler_params=_params(("parallel",), 48),
        name="final_rmsnorm",
    )(x, g.reshape(1, d))


def _prepare_weights(norm_attn, w_in, sink_a, q_norm_b, k_norm_b, norm_mem, w_mem_kv, w_branch_a, w_branch_b,
                     w_branch_m, w_out, rel_bias, norm_ffn, w_router, w_gate_e, w_up_e, w_down_e, norm_final):
    depth, d = norm_attn.shape
    w_router_t = jnp.swapaxes(w_router, 1, 2)
    wr_hi = w_router_t.astype(BF16)
    wr_lo = (w_router_t - wr_hi.astype(F32)).astype(BF16)
    return dict(
        norm_attn=norm_attn.reshape(depth, 1, d), norm_mem=norm_mem.reshape(depth, 1, d),
        norm_ffn=norm_ffn.reshape(depth, 1, d), norm_final=norm_final,
        q_norm_b=q_norm_b.reshape(depth, 1, HEAD_DIM), k_norm_b=k_norm_b.reshape(depth, 1, HEAD_DIM),
        sink_t=jnp.repeat(sink_a.reshape(depth, N_KV_A, 1, GROUP_A), BLOCK, axis=-1),
        bias_t=_window_bias_t(rel_bias),
        w_in=_reorder_w_in(w_in, d).astype(BF16),
        w_mem_kv=w_mem_kv.astype(BF16),
        w_branch_a=w_branch_a.astype(BF16), w_branch_b=w_branch_b.astype(BF16),
        w_branch_m=w_branch_m.astype(BF16), w_out=w_out.astype(BF16),
        wr_split=jnp.concatenate([wr_hi, wr_lo], axis=1),
        w_gate_e=w_gate_e.astype(BF16), w_up_e=w_up_e.astype(BF16), w_down_e=w_down_e.astype(BF16),
    )


def _encoder(x, mem, w):
    batch, seq, d = x.shape
    n_mem = mem.shape[1]
    n = batch * seq
    depth = w["w_in"].shape[0]
    n_exp = w["wr_split"].shape[1] // 2
    cap = max(1, CAPACITY_FACTOR * n // n_exp)
    off = _proj_layout(d)
    cos, sin = _rope_tables(seq)
    x = x.reshape(n, d)
    mem = mem.reshape(batch * n_mem, d)
    for l in range(depth):
        proj = _norm_matmul(x, w["norm_attn"], w["w_in"], l)
        q_rot, k_rot, va_t, vb_t = _attn_prep(proj, cos, sin, w["q_norm_b"], w["k_norm_b"], l, batch, seq, off)
        o_a = _window_attention(proj, va_t, w["sink_t"][l], w["bias_t"], batch, seq, off)
        o_b = _axial_attention(q_rot, k_rot, vb_t, batch, seq)
        mem_kv = _norm_matmul(mem, w["norm_mem"], w["w_mem_kv"], l)
        o_m = _memory_attention(proj, mem_kv, batch, seq, n_mem, off)
        merged = _branch_merge(o_a, o_b, o_m, proj, w["w_branch_a"], w["w_branch_b"], w["w_branch_m"], l, d)
        x, h_pack, aff_t = _out_proj(merged, x, w["w_out"], w["norm_ffn"], w["wr_split"], l)
        gate, idx = lax.top_k(aff_t, cap)
        x = _moe(x, h_pack, idx.astype(jnp.int32), gate, w["w_gate_e"], w["w_up_e"], w["w_down_e"], l)
    return _rmsnorm(x, w["norm_final"]).reshape(batch, seq, d)


def kernel(x_prompt, x_sample, mem_prompt, mem_sample, norm_attn, w_in, sink_a, q_norm_b, k_norm_b, norm_mem,
           w_mem_kv, w_branch_a, w_branch_b, w_branch_m, w_out, rel_bias, norm_ffn, w_router, w_gate_e, w_up_e,
           w_down_e, norm_final):
    w = _prepare_weights(norm_attn, w_in, sink_a, q_norm_b, k_norm_b, norm_mem, w_mem_kv, w_branch_a, w_branch_b,
                         w_branch_m, w_out, rel_bias, norm_ffn, w_router, w_gate_e, w_up_e, w_down_e, norm_final)
    y_prompt = _encoder(x_prompt, mem_prompt, w)
    y_sample = _encoder(x_sample, mem_sample, w)
    return (y_prompt, y_sample)
```

```python
import functools
import math

import jax
import jax.numpy as jnp
from jax import lax
from jax.experimental import pallas as pl
from jax.experimental.pallas import tpu as pltpu

F32 = jnp.float32
BF16 = jnp.bfloat16
U32 = jnp.uint32

HEAD_DIM = 128
N_HEADS_A = 6
N_KV_A = 2
GROUP_A = N_HEADS_A // N_KV_A
N_HEADS_B = 6
N_KV_B = 2
GROUP_B = N_HEADS_B // N_KV_B
N_HEADS_M = 4
N_BRANCH = 3
WINDOW = 128
BLOCK = 128
GRID_W = 64
ROPE_THETA = 10000.0
ROPE_AXIS_DIM = HEAD_DIM // 2
NUM_BUCKETS = 32
MAX_DISTANCE = 128
CAPACITY_FACTOR = 2
EPS = 1e-6
NEG = -1e30
ATTN_SCALE = HEAD_DIM ** -0.5
LOG2E = math.log2(math.e)

W_QA = N_HEADS_A * HEAD_DIM
W_KA = N_KV_A * HEAD_DIM
W_QB = N_HEADS_B * HEAD_DIM
W_KB = N_KV_B * HEAD_DIM
W_QM = N_HEADS_M * HEAD_DIM

MIB = 1024 * 1024
CHUNK = 512
SUBLANES = 8
DMA_UNROLL = 16


def _tile(n, pref):
    t = min(n, pref)
    assert n % t == 0, (n, t)
    return t


def _params(sem, vmem_mib):
    return pltpu.CompilerParams(dimension_semantics=sem, vmem_limit_bytes=vmem_mib * MIB)


def _dot(a, b):
    return jnp.dot(a, b, preferred_element_type=F32)


def _dot_t(a, b):
    return lax.dot_general(a, b, (((1,), (1,)), ((), ())), preferred_element_type=F32)


def _rms(x, g):
    return x * lax.rsqrt(jnp.mean(x * x, axis=-1, keepdims=True) + EPS) * g


def _norm_matmul_kernel(x_ref, g_ref, w_ref, o_ref, h_ref):
    @pl.when(pl.program_id(1) == 0)
    def _():
        h_ref[...] = _rms(x_ref[...], g_ref[...]).astype(h_ref.dtype)

    o_ref[...] = _dot(h_ref[...], w_ref[...]).astype(o_ref.dtype)


def _norm_matmul(x, g, w, l):
    n, d = x.shape
    m = w.shape[2]
    tm = _tile(n, 1024)
    tn = _tile(m, 1024)
    return pl.pallas_call(
        _norm_matmul_kernel,
        grid=(n // tm, m // tn),
        in_specs=[
            pl.BlockSpec((tm, d), lambda i, j: (i, 0)),
            pl.BlockSpec((None, 1, d), lambda i, j: (l, 0, 0)),
            pl.BlockSpec((None, d, tn), lambda i, j: (l, 0, j)),
        ],
        out_specs=pl.BlockSpec((tm, tn), lambda i, j: (i, j)),
        out_shape=jax.ShapeDtypeStruct((n, m), BF16),
        scratch_shapes=[pltpu.VMEM((tm, d), BF16)],
        compiler_params=_params(("parallel", "arbitrary"), 48),
        name="norm_matmul",
    )(x, g, w)


def _proj_layout(d):
    off = {}
    o = 0
    for name, width in (("gates", N_BRANCH * d), ("qa", W_QA), ("qb", W_QB), ("ka", W_KA), ("va", W_KA),
                        ("kb", W_KB), ("vb", W_KB), ("qm", W_QM)):
        off[name] = o
        o += width
    off["total"] = o
    return off


def _col_block(off, name, width):
    assert off[name] % width == 0, (name, width)
    return off[name] // width


def _reorder_w_in(w_in, d):
    s = [0]
    for width in (W_QA, W_KA, W_KA, W_QB, W_KB, W_KB, W_QM, N_BRANCH * d):
        s.append(s[-1] + width)
    qa, ka, va, qb, kb, vb, qm, gates = (w_in[..., s[i]:s[i + 1]] for i in range(8))
    return jnp.concatenate([gates, qa, qb, ka, va, kb, vb, qm], axis=-1)


def _attn_prep_kernel(q_ref, k_ref, va_ref, vb_ref, cos_ref, sin_ref, qg_ref, kg_ref, qo_ref, ko_ref, vat_ref,
                      vbt_ref):
    cos = cos_ref[...]
    sin = sin_ref[...]
    lane = lax.broadcasted_iota(jnp.int32, cos.shape, 1)
    half = ROPE_AXIS_DIM // 2
    first = (lane % ROPE_AXIS_DIM) < half

    def prep(x, g, scale):
        y = _rms(x.astype(F32), g)
        partner = jnp.where(first, pltpu.roll(y, HEAD_DIM - half, 1), pltpu.roll(y, half, 1))
        r = y * cos + partner * sin
        if scale is not None:
            r = r * scale
        return r.astype(BF16)

    for h in range(N_HEADS_B):
        sl = slice(h * HEAD_DIM, (h + 1) * HEAD_DIM)
        qo_ref[:, sl] = prep(q_ref[:, sl], qg_ref[...], ATTN_SCALE * LOG2E)
    for h in range(N_KV_B):
        sl = slice(h * HEAD_DIM, (h + 1) * HEAD_DIM)
        ko_ref[:, sl] = prep(k_ref[:, sl], kg_ref[...], None)
        vbt_ref[h] = vb_ref[:, sl].astype(F32).T.astype(BF16)
    for h in range(N_KV_A):
        sl = slice(h * HEAD_DIM, (h + 1) * HEAD_DIM)
        vat_ref[h] = va_ref[:, sl].astype(F32).T.astype(BF16)


def _rope_tables(seq):
    pos = jnp.arange(seq)
    row_pos = (pos // GRID_W).astype(F32)
    col_pos = (pos % GRID_W).astype(F32)
    freqs = ROPE_THETA ** (-jnp.arange(0, ROPE_AXIS_DIM, 2, dtype=F32) / ROPE_AXIS_DIM)
    ar = row_pos[:, None] * freqs[None, :]
    ac = col_pos[:, None] * freqs[None, :]
    cos = jnp.concatenate([jnp.cos(ar), jnp.cos(ar), jnp.cos(ac), jnp.cos(ac)], axis=-1)
    sin = jnp.concatenate([-jnp.sin(ar), jnp.sin(ar), -jnp.sin(ac), jnp.sin(ac)], axis=-1)
    return cos, sin


def _attn_prep(proj, cos, sin, q_gain, k_gain, l, batch, seq, off):
    n = proj.shape[0]
    tm = CHUNK
    assert seq % tm == 0
    ns = seq // tm
    vt_shape = jax.ShapeDtypeStruct((batch, N_KV_B, ns, HEAD_DIM, tm), BF16)
    vt_spec = pl.BlockSpec((None, N_KV_B, None, HEAD_DIM, tm), lambda i: (i // ns, 0, i % ns, 0, 0))
    return pl.pallas_call(
        _attn_prep_kernel,
        grid=(n // tm,),
        in_specs=[
            pl.BlockSpec((tm, W_QB), lambda i: (i, _col_block(off, "qb", W_QB))),
            pl.BlockSpec((tm, W_KB), lambda i: (i, _col_block(off, "kb", W_KB))),
            pl.BlockSpec((tm, W_KA), lambda i: (i, _col_block(off, "va", W_KA))),
            pl.BlockSpec((tm, W_KB), lambda i: (i, _col_block(off, "vb", W_KB))),
            pl.BlockSpec((tm, HEAD_DIM), lambda i: (i % ns, 0)),
            pl.BlockSpec((tm, HEAD_DIM), lambda i: (i % ns, 0)),
            pl.BlockSpec((None, 1, HEAD_DIM), lambda i: (l, 0, 0)),
            pl.BlockSpec((None, 1, HEAD_DIM), lambda i: (l, 0, 0)),
        ],
        out_specs=[
            pl.BlockSpec((tm, W_QB), lambda i: (i, 0)),
            pl.BlockSpec((tm, W_KB), lambda i: (i, 0)),
            vt_spec,
            vt_spec,
        ],
        out_shape=[jax.ShapeDtypeStruct((n, W_QB), BF16), jax.ShapeDtypeStruct((n, W_KB), BF16), vt_shape, vt_shape],
        compiler_params=_params(("parallel",), 32),
        name="attn_prep",
    )(proj, proj, proj, proj, cos, sin, q_gain, k_gain)


def _t5_bucket(rel):
    half = NUM_BUCKETS // 2
    max_exact = half // 2
    ret = jnp.where(rel > 0, half, 0)
    n = jnp.abs(rel)
    nf = jnp.maximum(n, 1).astype(F32)
    large = max_exact + (jnp.log(nf / max_exact) / math.log(MAX_DISTANCE / max_exact)
                         * (half - max_exact)).astype(jnp.int32)
    large = jnp.minimum(large, half - 1)
    return ret + jnp.where(n < max_exact, n, large)


def _window_bias_t(rel_bias):
    rel = jnp.arange(3 * BLOCK)[None, :] - BLOCK - jnp.arange(BLOCK)[:, None]
    bias = rel_bias[_t5_bucket(rel)].astype(F32)
    bias = bias.reshape(BLOCK, 3 * BLOCK, N_KV_A, GROUP_A).transpose(2, 1, 3, 0)
    return bias.reshape(N_KV_A, 3 * BLOCK, GROUP_A * BLOCK)


def _window_kernel(sink_ref, q_ref, kp_ref, kc_ref, kn_ref, vtp_ref, vtc_ref, vtn_ref, bias_ref, o_ref, *, nb, nsb):
    i = pl.program_id(1)
    w3 = 3 * BLOCK
    gw = GROUP_A * BLOCK
    key = lax.broadcasted_iota(jnp.int32, (w3, gw), 0)
    qpos = lax.broadcasted_iota(jnp.int32, (w3, gw), 1) % BLOCK
    band = jnp.abs(key - BLOCK - qpos) <= WINDOW
    for kv in range(N_KV_A):
        ks = slice(kv * HEAD_DIM, (kv + 1) * HEAD_DIM)
        kwin = jnp.concatenate([kp_ref[:, ks], kc_ref[:, ks], kn_ref[:, ks]], axis=0)
        vtwin = jnp.concatenate([vtp_ref[kv], vtc_ref[kv], vtn_ref[kv]], axis=1)
        sk = sink_ref[kv]
        for sb in range(nsb):
            gb = i * nsb + sb
            valid = band & ((key >= BLOCK) | (gb > 0)) & ((key < 2 * BLOCK) | (gb < nb - 1))
            rows = slice(sb * BLOCK, (sb + 1) * BLOCK)
            heads = [kv * GROUP_A + g for g in range(GROUP_A)]
            q3 = jnp.concatenate([q_ref[rows, h * HEAD_DIM:(h + 1) * HEAD_DIM] for h in heads], axis=0)
            st = _dot_t(kwin[sb * BLOCK:sb * BLOCK + w3], q3) * ATTN_SCALE
            st = jnp.where(valid, st + bias_ref[kv], NEG)
            m = jnp.maximum(jnp.max(st, axis=0, keepdims=True), sk)
            p = jnp.exp(st - m)
            inv = 1.0 / (jnp.sum(p, axis=0, keepdims=True) + jnp.exp(sk - m))
            ot = _dot(vtwin[:, sb * BLOCK:sb * BLOCK + w3], (p * inv).astype(BF16))
            for g, h in enumerate(heads):
                o_ref[rows, h * HEAD_DIM:(h + 1) * HEAD_DIM] = ot[:, g * BLOCK:(g + 1) * BLOCK].T.astype(o_ref.dtype)


def _window_attention(proj, va_t, sink_t, bias_t, batch, seq, off):
    n = proj.shape[0]
    tq = CHUNK
    nsb = tq // BLOCK
    nb = seq // BLOCK
    nt = seq // tq
    qblk = _col_block(off, "qa", W_QA)
    kblk = _col_block(off, "ka", W_KA)
    gw = GROUP_A * BLOCK
    return pl.pallas_call(
        functools.partial(_window_kernel, nb=nb, nsb=nsb),
        grid=(batch, nt),
        in_specs=[
            pl.BlockSpec((N_KV_A, 1, gw), lambda b, i: (0, 0, 0)),
            pl.BlockSpec((tq, W_QA), lambda b, i: (b * nt + i, qblk)),
            pl.BlockSpec((BLOCK, W_KA), lambda b, i: (b * nb + jnp.maximum(i * nsb - 1, 0), kblk)),
            pl.BlockSpec((tq, W_KA), lambda b, i: (b * nt + i, kblk)),
            pl.BlockSpec((BLOCK, W_KA), lambda b, i: (b * nb + jnp.minimum(i * nsb + nsb, nb - 1), kblk)),
            pl.BlockSpec((None, N_KV_A, None, HEAD_DIM, BLOCK),
                         lambda b, i: (b, 0, jnp.maximum(i - 1, 0), 0, nsb - 1)),
            pl.BlockSpec((None, N_KV_A, None, HEAD_DIM, tq), lambda b, i: (b, 0, i, 0, 0)),
            pl.BlockSpec((None, N_KV_A, None, HEAD_DIM, BLOCK),
                         lambda b, i: (b, 0, jnp.minimum(i + 1, nt - 1), 0, 0)),
            pl.BlockSpec((N_KV_A, 3 * BLOCK, gw), lambda b, i: (0, 0, 0)),
        ],
        out_specs=pl.BlockSpec((tq, W_QA), lambda b, i: (b * nt + i, 0)),
        out_shape=jax.ShapeDtypeStruct((n, W_QA), BF16),
        compiler_params=_params(("parallel", "parallel"), 32),
        name="window_attention",
    )(sink_t, proj, proj, proj, proj, va_t, va_t, va_t, bias_t)


def _axial_kernel(q_ref, k_ref, vt_ref, o_ref, qt_ref, st_a, st_b, m_ref, l_ref, acc_ref, *, tq, tk, nk):
    for g in range(GROUP_B):
        qt_ref[:, g * tq:(g + 1) * tq] = q_ref[:, g * HEAD_DIM:(g + 1) * HEAD_DIM].astype(F32).T.astype(BF16)
    m_ref[0] = jnp.full(m_ref.shape[1:], -jnp.inf, F32)
    l_ref[...] = jnp.zeros(l_ref.shape, F32)
    acc_ref[...] = jnp.zeros(acc_ref.shape, F32)

    def scores(j, st_ref, m_before):
        ks = pl.multiple_of(j * tk, tk)
        st = _dot(k_ref[pl.ds(ks, tk), :], qt_ref[...])
        st_ref[...] = st
        m_ref[j + 1] = jnp.maximum(m_before, jnp.max(st, axis=0, keepdims=True))

    def update(j, st_ref, m_prev, m_cur):
        alpha = jnp.exp2(m_prev - m_cur)
        pt = jnp.exp2(st_ref[...] - m_cur)
        l_ref[...] = alpha * l_ref[...] + jnp.sum(pt, axis=0, keepdims=True)
        acc_ref[...] = alpha * acc_ref[...] + _dot(vt_ref[j], pt.astype(BF16))

    def stage(j, st_cur, st_nxt, with_next):
        m_prev = m_ref[j]
        m_cur = m_ref[j + 1]
        if with_next:
            scores(j + 1, st_nxt, m_cur)
        update(j, st_cur, m_prev, m_cur)

    scores(0, st_a, m_ref[0])
    if nk > 1:
        assert nk % 2 == 0

        def body(p, carry):
            stage(2 * p, st_a, st_b, True)
            stage(2 * p + 1, st_b, st_a, True)
            return carry

        lax.fori_loop(0, nk // 2 - 1, body, 0)
        stage(nk - 2, st_a, st_b, True)
        stage(nk - 1, st_b, st_a, False)
    else:
        stage(0, st_a, st_b, False)
    ot = acc_ref[...] * (1.0 / l_ref[...])
    for g in range(GROUP_B):
        o_ref[:, g * HEAD_DIM:(g + 1) * HEAD_DIM] = ot[:, g * tq:(g + 1) * tq].T.astype(o_ref.dtype)


def _axial_attention(q_rot, k_rot, vb_t, batch, seq):
    n = q_rot.shape[0]
    tq = _tile(seq, 512)
    tk = CHUNK
    nq = seq // tq
    nk = seq // tk
    gw = GROUP_B * HEAD_DIM
    rows = GROUP_B * tq
    return pl.pallas_call(
        functools.partial(_axial_kernel, tq=tq, tk=tk, nk=nk),
        grid=(batch, N_KV_B, nq),
        in_specs=[
            pl.BlockSpec((tq, gw), lambda b, kv, i: (b * nq + i, kv)),
            pl.BlockSpec((seq, HEAD_DIM), lambda b, kv, i: (b, kv)),
            pl.BlockSpec((None, None, nk, HEAD_DIM, tk), lambda b, kv, i: (b, kv, 0, 0, 0)),
        ],
        out_specs=pl.BlockSpec((tq, gw), lambda b, kv, i: (b * nq + i, kv)),
        out_shape=jax.ShapeDtypeStruct((n, W_QB), BF16),
        scratch_shapes=[
            pltpu.VMEM((HEAD_DIM, rows), BF16),
            pltpu.VMEM((tk, rows), F32),
            pltpu.VMEM((tk, rows), F32),
            pltpu.VMEM((nk + 1, 1, rows), F32),
            pltpu.VMEM((1, rows), F32),
            pltpu.VMEM((HEAD_DIM, rows), F32),
        ],
        compiler_params=_params(("parallel", "parallel", "parallel"), 48),
        name="axial_attention",
    )(q_rot, k_rot, vb_t)


def _mem_kernel(q_ref, k_ref, v_ref, o_ref):
    for h in range(N_HEADS_M):
        hs = slice(h * HEAD_DIM, (h + 1) * HEAD_DIM)
        s = _dot_t(q_ref[:, hs], k_ref[:, hs]) * ATTN_SCALE
        m = jnp.max(s, axis=-1, keepdims=True)
        p = jnp.exp(s - m)
        p = p / jnp.sum(p, axis=-1, keepdims=True)
        o_ref[:, hs] = _dot(p.astype(BF16), v_ref[:, hs]).astype(o_ref.dtype)


def _memory_attention(proj, mem_kv, batch, seq, n_mem, off):
    n = proj.shape[0]
    tq = _tile(seq, 1024)
    nq = seq // tq
    qblk = _col_block(off, "qm", W_QM)
    return pl.pallas_call(
        _mem_kernel,
        grid=(batch, nq),
        in_specs=[
            pl.BlockSpec((tq, W_QM), lambda b, i: (b * nq + i, qblk)),
            pl.BlockSpec((n_mem, W_QM), lambda b, i: (b, 0)),
            pl.BlockSpec((n_mem, W_QM), lambda b, i: (b, 1)),
        ],
        out_specs=pl.BlockSpec((tq, W_QM), lambda b, i: (b * nq + i, 0)),
        out_shape=jax.ShapeDtypeStruct((n, W_QM), BF16),
        compiler_params=_params(("parallel", "parallel"), 32),
        name="memory_attention",
    )(proj, mem_kv, mem_kv)


def _sigmoid(x):
    return 0.5 * jnp.tanh(0.5 * x) + 0.5


def _branch_merge_kernel(oa_ref, ob_ref, om_ref, g0_ref, g1_ref, g2_ref, wa_ref, wb_ref, wm_ref, o_ref):
    t = _sigmoid(g0_ref[...].astype(F32)) * _dot(oa_ref[...], wa_ref[...])
    t = t + _sigmoid(g1_ref[...].astype(F32)) * _dot(ob_ref[...], wb_ref[...])
    t = t + _sigmoid(g2_ref[...].astype(F32)) * _dot(om_ref[...], wm_ref[...])
    o_ref[...] = t.astype(o_ref.dtype)


def _branch_merge(o_a, o_b, o_m, proj, wa, wb, wm, l, d):
    n = o_a.shape[0]
    tm = _tile(n, 1024)
    tn = _tile(d, 512)
    nj = d // tn
    return pl.pallas_call(
        _branch_merge_kernel,
        grid=(n // tm, nj),
        in_specs=[
            pl.BlockSpec((tm, W_QA), lambda i, j: (i, 0)),
            pl.BlockSpec((tm, W_QB), lambda i, j: (i, 0)),
            pl.BlockSpec((tm, W_QM), lambda i, j: (i, 0)),
            pl.BlockSpec((tm, tn), lambda i, j: (i, j)),
            pl.BlockSpec((tm, tn), lambda i, j: (i, nj + j)),
            pl.BlockSpec((tm, tn), lambda i, j: (i, 2 * nj + j)),
            pl.BlockSpec((None, W_QA, tn), lambda i, j: (l, 0, j)),
            pl.BlockSpec((None, W_QB, tn), lambda i, j: (l, 0, j)),
            pl.BlockSpec((None, W_QM, tn), lambda i, j: (l, 0, j)),
        ],
        out_specs=pl.BlockSpec((tm, tn), lambda i, j: (i, j)),
        out_shape=jax.ShapeDtypeStruct((n, d), BF16),
        compiler_params=_params(("parallel", "arbitrary"), 48),
        name="branch_merge",
    )(o_a, o_b, o_m, proj, proj, proj, wa, wb, wm)


def _out_proj_kernel(m_ref, x_ref, wo_ref, nf_ref, wr_ref, xo_ref, hp_ref, aff_ref):
    n_exp = aff_ref.shape[0]
    half = hp_ref.shape[1]
    xn = x_ref[...] + _dot(m_ref[...], wo_ref[...])
    xo_ref[...] = xn
    h = _rms(xn, nf_ref[...])
    h_hi = h.astype(BF16)
    h_hi32 = h_hi.astype(F32)
    h_lo = (h - h_hi32).astype(BF16)
    r_hi = _dot_t(wr_ref[...], h_hi)
    r_lo = _dot_t(wr_ref[...], h_lo)
    logits = r_hi[:n_exp] + r_hi[n_exp:] + r_lo[:n_exp]
    e = jnp.exp(logits - jnp.max(logits, axis=0, keepdims=True))
    aff_ref[...] = e / jnp.sum(e, axis=0, keepdims=True)
    bits = lax.bitcast_convert_type(h_hi32, U32)
    hp_ref[...] = (bits[:, :half] >> 16) | (bits[:, half:] & jnp.uint32(0xFFFF0000))


def _out_proj(merged, x, wo, norm_ffn, wr_split, l):
    n, d = x.shape
    n_exp = wr_split.shape[1] // 2
    tm = _tile(n, 512)
    return pl.pallas_call(
        _out_proj_kernel,
        grid=(n // tm,),
        in_specs=[
            pl.BlockSpec((tm, d), lambda i: (i, 0)),
            pl.BlockSpec((tm, d), lambda i: (i, 0)),
            pl.BlockSpec((None, d, d), lambda i: (l, 0, 0), pipeline_mode=pl.Buffered(1)),
            pl.BlockSpec((None, 1, d), lambda i: (l, 0, 0)),
            pl.BlockSpec((None, 2 * n_exp, d), lambda i: (l, 0, 0)),
        ],
        out_specs=[
            pl.BlockSpec((tm, d), lambda i: (i, 0)),
            pl.BlockSpec((tm, d // 2), lambda i: (i, 0)),
            pl.BlockSpec((n_exp, tm), lambda i: (0, i)),
        ],
        out_shape=[
            jax.ShapeDtypeStruct((n, d), F32),
            jax.ShapeDtypeStruct((n, d // 2), U32),
            jax.ShapeDtypeStruct((n_exp, n), F32),
        ],
        input_output_aliases={1: 0},
        compiler_params=_params(("parallel",), 56),
        name="out_proj",
    )(merged, x, wo, norm_ffn, wr_split)


def _moe_kernel(idx_ref, gate_ref, wg_ref, wu_ref, wd_ref, hp_hbm, x_in_hbm, x_hbm, xe_pack, xe_bf16, xrow, acc,
                sem_h, sem_x, sem_s, *, tc, nc, nf):
    del x_in_hbm
    e = pl.program_id(0)
    c = pl.program_id(1)
    f = pl.program_id(2)
    t = e * nc + c
    last_t = pl.num_programs(0) * nc - 1
    slot = t % 2
    half = xe_pack.shape[3]
    per_step = tc // nf

    def h_copy(pos, dst_slot, kg, ks):
        return pltpu.make_async_copy(hp_hbm.at[pl.ds(idx_ref[pos], 1)], xe_pack.at[dst_slot, kg, pl.ds(ks, 1)],
                                     sem_h.at[dst_slot])

    def wait_rows(buf, sem):
        pltpu.make_async_copy(buf, buf, sem).wait()

    def wait_h(dst_slot):
        wait_rows(xe_pack.at[dst_slot], sem_h.at[dst_slot])

    @pl.when(f == 0)
    def _():
        @pl.when(t == 0)
        def _():
            def body(i, carry):
                for j in range(DMA_UNROLL):
                    h_copy(i * DMA_UNROLL + j, 0, i * (DMA_UNROLL // SUBLANES) + j // SUBLANES, j % SUBLANES).start()
                return carry

            lax.fori_loop(0, tc // DMA_UNROLL, body, 0)

        @pl.when(t > 0)
        def _():
            wait_rows(xrow, sem_s)

        wait_h(slot)
        w = xe_pack[slot]
        lo = lax.bitcast_convert_type(w << 16, F32)
        hi = lax.bitcast_convert_type(w & jnp.uint32(0xFFFF0000), F32)
        xe_bf16[:, :half] = lo.reshape(tc, half).astype(BF16)
        xe_bf16[:, half:] = hi.reshape(tc, half).astype(BF16)
        acc[...] = jnp.zeros(acc.shape, F32)

    next_base = jnp.minimum(t + 1, last_t) * tc + f * per_step
    this_base = t * tc + f * per_step
    for j in range(per_step):
        kg = f * (per_step // SUBLANES) + j // SUBLANES
        h_copy(next_base + j, 1 - slot, kg, j % SUBLANES).start()
        pltpu.make_async_copy(x_hbm.at[pl.ds(idx_ref[this_base + j], 1)], xrow.at[kg, pl.ds(j % SUBLANES, 1)],
                              sem_x).start()

    xe = xe_bf16[...]
    a = _dot(xe, wg_ref[...])
    b = _dot(xe, wu_ref[...])
    acc[...] += _dot((jax.nn.silu(a) * b).astype(BF16), wd_ref[...])

    @pl.when(f == nf - 1)
    def _():
        wait_rows(xrow, sem_x)
        xrow[...] = xrow[...] + (acc[...] * gate_ref[...]).reshape(xrow.shape)

        def body(i, carry):
            for j in range(DMA_UNROLL):
                r = idx_ref[t * tc + i * DMA_UNROLL + j]
                kg = i * (DMA_UNROLL // SUBLANES) + j // SUBLANES
                pltpu.make_async_copy(xrow.at[kg, pl.ds(j % SUBLANES, 1)], x_hbm.at[pl.ds(r, 1)],
                                      sem_s).start(priority=j % 2)
            return carry

        lax.fori_loop(0, tc // DMA_UNROLL, body, 0)

        @pl.when(t == last_t)
        def _():
            wait_rows(xrow, sem_s)
            wait_h(1 - slot)


def _moe(x, h_pack, idx, gate, wg, wu, wd, l):
    n, d = x.shape
    n_exp, cap = idx.shape
    dff = wg.shape[3]
    tc = _tile(cap, 1024)
    tf = _tile(dff, 512)
    nc = cap // tc
    nf = dff // tf
    assert tc % DMA_UNROLL == 0 and DMA_UNROLL % SUBLANES == 0
    assert tc % nf == 0 and (tc // nf) % SUBLANES == 0
    return pl.pallas_call(
        functools.partial(_moe_kernel, tc=tc, nc=nc, nf=nf),
        grid_spec=pltpu.PrefetchScalarGridSpec(
            num_scalar_prefetch=1,
            grid=(n_exp, nc, nf),
            in_specs=[
                pl.BlockSpec((tc, 1), lambda e, c, f, idx: (e * nc + c, 0)),
                pl.BlockSpec((None, None, d, tf), lambda e, c, f, idx: (l, e, 0, f)),
                pl.BlockSpec((None, None, d, tf), lambda e, c, f, idx: (l, e, 0, f)),
                pl.BlockSpec((None, None, tf, d), lambda e, c, f, idx: (l, e, f, 0)),
                pl.BlockSpec(memory_space=pl.ANY),
                pl.BlockSpec(memory_space=pl.ANY),
            ],
            out_specs=pl.BlockSpec(memory_space=pl.ANY),
            scratch_shapes=[
                pltpu.VMEM((2, tc // SUBLANES, SUBLANES, d // 2), U32),
                pltpu.VMEM((tc, d), BF16),
                pltpu.VMEM((tc // SUBLANES, SUBLANES, d), F32),
                pltpu.VMEM((tc, d), F32),
                pltpu.SemaphoreType.DMA((2,)),
                pltpu.SemaphoreType.DMA(()),
                pltpu.SemaphoreType.DMA(()),
            ],
        ),
        out_shape=jax.ShapeDtypeStruct((n, d), F32),
        input_output_aliases={6: 0},
        compiler_params=_params(("arbitrary", "arbitrary", "arbitrary"), 56),
        name="moe_experts",
    )(idx.reshape(-1), gate.reshape(-1, 1), wg, wu, wd, h_pack, x)


def _rmsnorm_kernel(x_ref, g_ref, o_ref):
    o_ref[...] = _rms(x_ref[...], g_ref[...])


def _rmsnorm(x, g):
    n, d = x.shape
    tm = _tile(n, 1024)
    return pl.pallas_call(
        _rmsnorm_kernel,
        grid=(n // tm,),
        in_specs=[pl.BlockSpec((tm, d), lambda i: (i, 0)), pl.BlockSpec((1, d), lambda i: (0, 0))],
        out_specs=pl.BlockSpec((tm, d), lambda i: (i, 0)),
        out_shape=jax.ShapeDtypeStruct((n, d), F32),
        compiler_params=_params(("parallel",), 48),
        name="final_rmsnorm",
    )(x, g.reshape(1, d))


def _prepare_weights(norm_attn, w_in, sink_a, q_norm_b, k_norm_b, norm_mem, w_mem_kv, w_branch_a, w_branch_b,
                     w_branch_m, w_out, rel_bias, norm_ffn, w_router, w_gate_e, w_up_e, w_down_e, norm_final):
    depth, d = norm_attn.shape
    w_router_t = jnp.swapaxes(w_router, 1, 2)
    wr_hi = w_router_t.astype(BF16)
    wr_lo = (w_router_t - wr_hi.astype(F32)).astype(BF16)
    return dict(
        norm_attn=norm_attn.reshape(depth, 1, d), norm_mem=norm_mem.reshape(depth, 1, d),
        norm_ffn=norm_ffn.reshape(depth, 1, d), norm_final=norm_final,
        q_norm_b=q_norm_b.reshape(depth, 1, HEAD_DIM), k_norm_b=k_norm_b.reshape(depth, 1, HEAD_DIM),
        sink_t=jnp.repeat(sink_a.reshape(depth, N_KV_A, 1, GROUP_A), BLOCK, axis=-1),
        bias_t=_window_bias_t(rel_bias),
        w_in=_reorder_w_in(w_in, d).astype(BF16),
        w_mem_kv=w_mem_kv.astype(BF16),
        w_branch_a=w_branch_a.astype(BF16), w_branch_b=w_branch_b.astype(BF16),
        w_branch_m=w_branch_m.astype(BF16), w_out=w_out.astype(BF16),
        wr_split=jnp.concatenate([wr_hi, wr_lo], axis=1),
        w_gate_e=w_gate_e.astype(BF16), w_up_e=w_up_e.astype(BF16), w_down_e=w_down_e.astype(BF16),
    )


def _encoder(x, mem, w):
    batch, seq, d = x.shape
    n_mem = mem.shape[1]
    n = batch * seq
    depth = w["w_in"].shape[0]
    n_exp = w["wr_split"].shape[1] // 2
    cap = max(1, CAPACITY_FACTOR * n // n_exp)
    off = _proj_layout(d)
    cos, sin = _rope_tables(seq)
    x = x.reshape(n, d)
    mem = mem.reshape(batch * n_mem, d)
    for l in range(depth):
        proj = _norm_matmul(x, w["norm_attn"], w["w_in"], l)
        q_rot, k_rot, va_t, vb_t = _attn_prep(proj, cos, sin, w["q_norm_b"], w["k_norm_b"], l, batch, seq, off)
        o_a = _window_attention(proj, va_t, w["sink_t"][l], w["bias_t"], batch, seq, off)
        o_b = _axial_attention(q_rot, k_rot, vb_t, batch, seq)
        mem_kv = _norm_matmul(mem, w["norm_mem"], w["w_mem_kv"], l)
        o_m = _memory_attention(proj, mem_kv, batch, seq, n_mem, off)
        merged = _branch_merge(o_a, o_b, o_m, proj, w["w_branch_a"], w["w_branch_b"], w["w_branch_m"], l, d)
        x, h_pack, aff_t = _out_proj(merged, x, w["w_out"], w["norm_ffn"], w["wr_split"], l)
        gate, idx = lax.top_k(aff_t, cap)
        x = _moe(x, h_pack, idx.astype(jnp.int32), gate, w["w_gate_e"], w["w_up_e"], w["w_down_e"], l)
    return _rmsnorm(x, w["norm_final"]).reshape(batch, seq, d)


def kernel(x_prompt, x_sample, mem_prompt, mem_sample, norm_attn, w_in, sink_a, q_norm_b, k_norm_b, norm_mem,
           w_mem_kv, w_branch_a, w_branch_b, w_branch_m, w_out, rel_bias, norm_ffn, w_router, w_gate_e, w_up_e,
           w_down_e, norm_final):
    w = _prepare_weights(norm_attn, w_in, sink_a, q_norm_b, k_norm_b, norm_mem, w_mem_kv, w_branch_a, w_branch_b,
                         w_branch_m, w_out, rel_bias, norm_ffn, w_router, w_gate_e, w_up_e, w_down_e, norm_final)
    y_prompt = _encoder(x_prompt, mem_prompt, w)
    y_sample = _encoder(x_sample, mem_sample, w)
    return (y_prompt, y_sample)
```
